```python
import math
import jax, jax.numpy as jnp
from jax import lax
import numpy as np

D_MODEL = 1024
BATCH = 8
SEQ = 4096
DEPTH = 4

N_MEM = 256
SSM_WIDTH = D_MODEL // 2
SSM_GROUP = 16
SSM_GROUPS = SSM_WIDTH // SSM_GROUP
SSM_STATE = 64
DT_MIN = 1e-3
DT_MAX = 1e-1
POOL_WIDTH = D_MODEL // 2
POOL_WINDOWS = (2, 4, 8, 16)
POOL_GROUP = POOL_WIDTH // len(POOL_WINDOWS)
IN_COLS = SSM_WIDTH + POOL_WIDTH + 2 * D_MODEL
N_XHEADS = 4
XHEAD_DIM = D_MODEL // N_XHEADS
D_FF_DENSE = ((8 * D_MODEL // 3 + 127) // 128) * 128
N_EXPERTS = 8
TOP_K = 2
D_FF_EXPERT = 7 * D_MODEL // 2
N_DENSE = (DEPTH + 1) // 2
N_MOE = DEPTH // 2
ALPHA = (2 * DEPTH) ** 0.25
BETA = (8 * DEPTH) ** -0.25
LN_EPS = 1e-5

kernel_name = 'hybrid_s5_pool_xattn_moe_deepnorm'


def layer_norm(x, g, b):
    xf = x.astype(jnp.float32)
    mu = jnp.mean(xf, axis=-1, keepdims=True)
    xc = xf - mu
    var = jnp.mean(xc * xc, axis=-1, keepdims=True)
    return (xc * lax.rsqrt(var + LN_EPS)).astype(x.dtype) * g + b


def _complex_affine_combine(earlier, later):
    a1r, a1i, b1r, b1i = earlier
    a2r, a2i, b2r, b2i = later
    ar = a2r * a1r - a2i * a1i
    ai = a2r * a1i + a2i * a1r
    br = a2r * b1r - a2i * b1i + b2r
    bi = a2r * b1i + a2i * b1r + b2i
    return (ar, ai, br, bi)


def ssm_branch(u, a_re, a_im, log_dt, b_re, b_im, c_re, c_im, d, w_glu, b_glu):
    dtype = u.dtype
    bsz, seq, _ = u.shape
    f32 = jnp.float32
    uf = u.astype(f32)
    ug = uf.reshape(bsz, seq, SSM_GROUPS, SSM_GROUP)
    a_re, a_im = a_re.astype(f32), a_im.astype(f32)
    dt = jnp.exp(log_dt.astype(f32))[:, None]
    mag = jnp.exp(a_re * dt)
    lam_re = mag * jnp.cos(a_im * dt)
    lam_im = mag * jnp.sin(a_im * dt)
    den = a_re * a_re + a_im * a_im
    num_re = lam_re - 1.0
    f_re = (num_re * a_re + lam_im * a_im) / den
    f_im = (lam_im * a_re - num_re * a_im) / den
    b_re, b_im = b_re.astype(f32), b_im.astype(f32)
    bb_re = f_re[..., None] * b_re - f_im[..., None] * b_im
    bb_im = f_re[..., None] * b_im + f_im[..., None] * b_re
    bu_re = jnp.einsum('bsgc,gpc->bsgp', ug, bb_re)
    bu_im = jnp.einsum('bsgc,gpc->bsgp', ug, bb_im)
    a_seq_re = jnp.broadcast_to(lam_re, (1, seq, SSM_GROUPS, SSM_STATE))
    a_seq_im = jnp.broadcast_to(lam_im, (1, seq, SSM_GROUPS, SSM_STATE))
    _, _, h_re, h_im = lax.associative_scan(
        _complex_affine_combine, (a_seq_re, a_seq_im, bu_re, bu_im), axis=1)
    y = (jnp.einsum('bsgp,gcp->bsgc', h_re, c_re.astype(f32))
         - jnp.einsum('bsgp,gcp->bsgc', h_im, c_im.astype(f32)))
    y = y.reshape(bsz, seq, SSM_WIDTH) + d.astype(f32) * uf
    y = jax.nn.gelu(y)
    y = y * jax.nn.sigmoid(y @ w_glu.astype(f32) + b_glu.astype(f32))
    return y.astype(dtype)


def pool_branch(u, pool_w, pool_scale):
    dtype = u.dtype
    bsz, seq, _ = u.shape
    uf = u.astype(jnp.float32)
    cs = jnp.cumsum(uf, axis=1)
    pos = jnp.arange(1, seq + 1, dtype=jnp.float32)
    outs = []
    for gi, win in enumerate(POOL_WINDOWS):
        sl = slice(gi * POOL_GROUP, (gi + 1) * POOL_GROUP)
        c = cs[..., sl]
        prev = jnp.pad(c[:, :-win], ((0, 0), (win, 0), (0, 0)))
        mean = (c - prev) / jnp.minimum(pos, float(win))[None, :, None]
        outs.append(mean - uf[..., sl])
    y = jnp.stack(outs, axis=2)
    y = jnp.einsum('bsgc,gcd->bsgd', y, pool_w.astype(jnp.float32))
    y = y.reshape(bsz, seq, POOL_WIDTH) * pool_scale.astype(jnp.float32)
    return y.astype(dtype)


def hybrid_mixer(x, w_in, a_re, a_im, log_dt, b_re, b_im, c_re, c_im, d,
                 w_glu, b_glu, w_ssm_up, pool_w, pool_scale, w_pool_up, w_out):
    proj = x @ w_in
    u_ssm = proj[..., :SSM_WIDTH]
    u_pool = proj[..., SSM_WIDTH:SSM_WIDTH + POOL_WIDTH]
    gates = jax.nn.sigmoid(proj[..., SSM_WIDTH + POOL_WIDTH:])
    g_ssm, g_pool = gates[..., :D_MODEL], gates[..., D_MODEL:]
    y_ssm = ssm_branch(u_ssm, a_re, a_im, log_dt, b_re, b_im, c_re, c_im,
                       d, w_glu, b_glu) @ w_ssm_up
    y_pool = pool_branch(u_pool, pool_w, pool_scale) @ w_pool_up
    return (g_ssm * y_ssm + g_pool * y_pool) @ w_out


def cross_attention(x, mem, wq, wk, wv, wo):
    bsz, seq, _ = x.shape
    q = (x @ wq).reshape(bsz, seq, N_XHEADS, XHEAD_DIM)
    k = (mem @ wk).reshape(bsz, -1, N_XHEADS, XHEAD_DIM)
    v = (mem @ wv).reshape(bsz, -1, N_XHEADS, XHEAD_DIM)
    scores = jnp.einsum('bshd,bmhd->bhsm', q, k) * (XHEAD_DIM ** -0.5)
    probs = jax.nn.softmax(scores.astype(jnp.float32), axis=-1).astype(v.dtype)
    out = jnp.einsum('bhsm,bmhd->bshd', probs, v).reshape(bsz, seq, D_MODEL)
    return out @ wo


def swiglu_ffn(x, w1, w3, w2):
    return (jax.nn.silu(x @ w1) * (x @ w3)) @ w2


def moe_ffn(x, router, w1, w3, w2):
    bsz, seq, d = x.shape
    xt = x.reshape(-1, d)
    logits = (xt @ router).astype(jnp.float32)
    top_val, top_idx = lax.top_k(logits, TOP_K)
    top_w = jax.nn.softmax(top_val, axis=-1)
    gates = jnp.sum(jax.nn.one_hot(top_idx, N_EXPERTS, dtype=jnp.float32)
                    * top_w[..., None], axis=1).astype(x.dtype)
    out = jnp.zeros_like(xt)
    for e in range(N_EXPERTS):
        out = out + gates[:, e:e + 1] * swiglu_ffn(xt, w1[e], w3[e], w2[e])
    return out.reshape(bsz, seq, d)


def setup_inputs(seed: int = 0) -> dict:
    key = jax.random.key(seed)
    ks = jax.random.split(key, 40)
    f32 = jnp.float32
    L, G, P, C, D = DEPTH, SSM_GROUPS, SSM_STATE, SSM_GROUP, D_MODEL

    def nrm(i, shape, scale):
        return scale * jax.random.normal(ks[i], shape, f32)

    a_im_init = jnp.broadcast_to(math.pi * jnp.arange(P, dtype=f32), (L, G, P))
    return {
        'x': nrm(0, (BATCH, SEQ, D), 1.0),
        'mem': nrm(1, (BATCH, N_MEM, D), 1.0),
        'ln_mix_g': 1.0 + nrm(2, (L, D), 0.02),
        'ln_mix_b': nrm(3, (L, D), 0.02),
        'w_in': nrm(4, (L, D, IN_COLS), D ** -0.5),
        'ssm_a_re': -0.5 + nrm(5, (L, G, P), 0.01),
        'ssm_a_im': a_im_init + nrm(6, (L, G, P), 0.01),
        'ssm_log_dt': jax.random.uniform(ks[7], (L, G), f32, math.log(DT_MIN), math.log(DT_MAX)),
        'ssm_b_re': nrm(8, (L, G, P, C), (2 * C) ** -0.5),
        'ssm_b_im': nrm(9, (L, G, P, C), (2 * C) ** -0.5),
        'ssm_c_re': nrm(10, (L, G, C, P), (2 * P) ** -0.5),
        'ssm_c_im': nrm(11, (L, G, C, P), (2 * P) ** -0.5),
        'ssm_d': nrm(12, (L, SSM_WIDTH), 1.0),
        'ssm_w_glu': nrm(13, (L, SSM_WIDTH, SSM_WIDTH), SSM_WIDTH ** -0.5),
        'ssm_b_glu': nrm(14, (L, SSM_WIDTH), 0.01),
        'w_ssm_up': nrm(15, (L, SSM_WIDTH, D), BETA * SSM_WIDTH ** -0.5),
        'pool_w': nrm(16, (L, len(POOL_WINDOWS), POOL_GROUP, POOL_GROUP), POOL_GROUP ** -0.5),
        'pool_scale': 1.0 + nrm(17, (L, POOL_WIDTH), 0.02),
        'w_pool_up': nrm(18, (L, POOL_WIDTH, D), BETA * POOL_WIDTH ** -0.5),
        'w_out': nrm(19, (L, D, D), BETA * D ** -0.5),
        'ln_xa_g': 1.0 + nrm(20, (L, D), 0.02),
        'ln_xa_b': nrm(21, (L, D), 0.02),
        'xa_wq': nrm(22, (L, D, D), D ** -0.5),
        'xa_wk': nrm(23, (L, D, D), D ** -0.5),
        'xa_wv': nrm(24, (L, D, D), BETA * D ** -0.5),
        'xa_wo': nrm(25, (L, D, D), BETA * D ** -0.5),
        'ln_ffn_g': 1.0 + nrm(26, (L, D), 0.02),
        'ln_ffn_b': nrm(27, (L, D), 0.02),
        'ffn_w1': nrm(28, (N_DENSE, D, D_FF_DENSE), BETA * D ** -0.5),
        'ffn_w3': nrm(29, (N_DENSE, D, D_FF_DENSE), BETA * D ** -0.5),
        'ffn_w2': nrm(30, (N_DENSE, D_FF_DENSE, D), BETA * D_FF_DENSE ** -0.5),
        'moe_router': nrm(31, (N_MOE, D, N_EXPERTS), D ** -0.5),
        'moe_w1': nrm(32, (N_MOE, N_EXPERTS, D, D_FF_EXPERT), BETA * D ** -0.5),
        'moe_w3': nrm(33, (N_MOE, N_EXPERTS, D, D_FF_EXPERT), BETA * D ** -0.5),
        'moe_w2': nrm(34, (N_MOE, N_EXPERTS, D_FF_EXPERT, D), BETA * D_FF_EXPERT ** -0.5),
    }


def reference(x, mem, ln_mix_g, ln_mix_b, w_in, ssm_a_re, ssm_a_im, ssm_log_dt,
              ssm_b_re, ssm_b_im, ssm_c_re, ssm_c_im, ssm_d, ssm_w_glu, ssm_b_glu,
              w_ssm_up, pool_w, pool_scale, w_pool_up, w_out, ln_xa_g, ln_xa_b,
              xa_wq, xa_wk, xa_wv, xa_wo, ln_ffn_g, ln_ffn_b, ffn_w1, ffn_w3, ffn_w2,
              moe_router, moe_w1, moe_w3, moe_w2):
    for l in range(DEPTH):
        y = hybrid_mixer(x, w_in[l], ssm_a_re[l], ssm_a_im[l], ssm_log_dt[l],
                         ssm_b_re[l], ssm_b_im[l], ssm_c_re[l], ssm_c_im[l], ssm_d[l],
                         ssm_w_glu[l], ssm_b_glu[l], w_ssm_up[l], pool_w[l],
                         pool_scale[l], w_pool_up[l], w_out[l])
        x = layer_norm(ALPHA * x + y, ln_mix_g[l], ln_mix_b[l])
        y = cross_attention(x, mem, xa_wq[l], xa_wk[l], xa_wv[l], xa_wo[l])
        x = layer_norm(ALPHA * x + y, ln_xa_g[l], ln_xa_b[l])
        if l % 2 == 0:
            y = swiglu_ffn(x, ffn_w1[l // 2], ffn_w3[l // 2], ffn_w2[l // 2])
        else:
            y = moe_ffn(x, moe_router[l // 2], moe_w1[l // 2], moe_w3[l // 2], moe_w2[l // 2])
        x = layer_norm(ALPHA * x + y, ln_ffn_g[l], ln_ffn_b[l])
    return x
```

```python
import functools
import math

import jax
import jax.numpy as jnp
from jax import lax
from jax.experimental import pallas as pl
from jax.experimental.pallas import tpu as pltpu

F32 = jnp.float32
BF16 = jnp.bfloat16

LN_EPS = 1e-5
SSM_GROUP = 16
SSM_STATE = 64
GROUPS_PER_BLOCK = 8
SCAN_ROWS = 8
POOL_WINDOWS = (2, 4, 8, 16)
POOL_HALO = 16
N_XHEADS = 4
N_EXPERTS = 8
LANES = 128

VMEM_LIMIT_BYTES = 56 * 1024 * 1024

TM_PROJ = 512
TS_SCAN = 256
TM_ATTN = 512
TM_FFN = 512
TM_ROUTER = 512
TM_EXPERT = 1024
TM_COMBINE = 256
GATHER_CHUNK = 1024


def _params(*sem):
    return pltpu.CompilerParams(dimension_semantics=sem, vmem_limit_bytes=VMEM_LIMIT_BYTES)


def _layer_norm(z, g, b):
    mu = jnp.mean(z, axis=-1, keepdims=True)
    zc = z - mu
    var = jnp.mean(zc * zc, axis=-1, keepdims=True)
    return zc * lax.rsqrt(var + LN_EPS) * g + b


def _sigmoid(v):
    return 1.0 / (1.0 + jnp.exp(-v))


def _bdot(a, b):
    return jnp.dot(a, b, preferred_element_type=F32)


def _ssm_prep_kernel(a_re_ref, a_im_ref, log_dt_ref, b_re_ref, b_im_ref,
                     pos_re_ref, pos_im_ref, neg_re_ref, neg_im_ref, bb_re_ref, bb_im_ref):
    a_re = a_re_ref[...]
    a_im = a_im_ref[...]
    dt = jnp.exp(log_dt_ref[...])
    mag = jnp.exp(a_re * dt)
    lam_re = mag * jnp.cos(a_im * dt)
    lam_im = mag * jnp.sin(a_im * dt)
    den = a_re * a_re + a_im * a_im
    num_re = lam_re - 1.0
    f_re = (num_re * a_re + lam_im * a_im) / den
    f_im = (lam_im * a_re - num_re * a_im) / den
    b_re = b_re_ref[...]
    b_im = b_im_ref[...]
    bb_re_ref[...] = f_re[:, None, :] * b_re - f_im[:, None, :] * b_im
    bb_im_ref[...] = f_re[:, None, :] * b_im + f_im[:, None, :] * b_re
    m2 = lam_re * lam_re + lam_im * lam_im
    inv_re = lam_re / m2
    inv_im = -lam_im / m2
    p_re, p_im = lam_re, lam_im
    n_re, n_im = inv_re, inv_im
    for k in range(SCAN_ROWS):
        pos_re_ref[k] = p_re
        pos_im_ref[k] = p_im
        neg_re_ref[k] = n_re
        neg_im_ref[k] = n_im
        p_re, p_im = p_re * lam_re - p_im * lam_im, p_re * lam_im + p_im * lam_re
        n_re, n_im = n_re * inv_re - n_im * inv_im, n_re * inv_im + n_im * inv_re


def _ssm_prep(a_re, a_im, log_dt, b_re, b_im):
    nl, g, p = a_re.shape
    c = b_re.shape[-1]
    lg = nl * g
    tab = jax.ShapeDtypeStruct((SCAN_ROWS, lg, p), F32)
    bbs = jax.ShapeDtypeStruct((lg, c, p), F32)
    outs = pl.pallas_call(
        _ssm_prep_kernel,
        out_shape=(tab, tab, tab, tab, bbs, bbs),
        name="ssm_prep",
    )(a_re.reshape(lg, p), a_im.reshape(lg, p), log_dt.reshape(lg, 1),
      b_re.transpose(0, 1, 3, 2).reshape(lg, c, p), b_im.transpose(0, 1, 3, 2).reshape(lg, c, p))
    pos_re, pos_im, neg_re, neg_im, bb_re, bb_im = outs

    def tab_layout(t):
        return t.reshape(SCAN_ROWS, nl, g * p).transpose(1, 0, 2)

    return (tab_layout(pos_re), tab_layout(pos_im), tab_layout(neg_re), tab_layout(neg_im),
            bb_re.reshape(nl, g, c, p), bb_im.reshape(nl, g, c, p))


def _block_diag_in(bb_re, bb_im):
    g, c, p = bb_re.shape
    nb = g // GROUPS_PER_BLOCK
    eye = jnp.eye(GROUPS_PER_BLOCK, dtype=F32)

    def one(bb):
        bb = bb.reshape(nb, GROUPS_PER_BLOCK, c, p)
        m = bb[:, :, :, None, :] * eye[None, :, None, :, None]
        return m.reshape(nb, GROUPS_PER_BLOCK * c, GROUPS_PER_BLOCK * p)

    return jnp.concatenate([one(bb_re), one(bb_im)], axis=-1).astype(BF16)


def _block_diag_out(c_re, c_im):
    g, c, p = c_re.shape
    nb = g // GROUPS_PER_BLOCK
    eye = jnp.eye(GROUPS_PER_BLOCK, dtype=F32)

    def one(cm):
        cm = cm.reshape(nb, GROUPS_PER_BLOCK, c, p).transpose(0, 1, 3, 2)
        m = cm[:, :, :, None, :] * eye[None, :, None, :, None]
        return m.reshape(nb, GROUPS_PER_BLOCK * p, GROUPS_PER_BLOCK * c)

    return jnp.concatenate([one(c_re), -one(c_im)], axis=1).astype(BF16)


def _mix_in_kernel(x_ref, w_ref, us_ref, up_ref, g_ref, *, ssm_w, pool_w):
    xb = x_ref[...].astype(BF16)
    us_ref[...] = _bdot(xb, w_ref[:, :ssm_w])
    up_ref[...] = _bdot(xb, w_ref[:, ssm_w:ssm_w + pool_w])
    g_ref[...] = _sigmoid(_bdot(xb, w_ref[:, ssm_w + pool_w:])).astype(BF16)


def _mix_in(x, w_in, ssm_w, pool_w):
    t, d = x.shape
    cols = w_in.shape[1]
    gate_w = cols - ssm_w - pool_w
    tm = TM_PROJ
    return pl.pallas_call(
        functools.partial(_mix_in_kernel, ssm_w=ssm_w, pool_w=pool_w),
        grid=(t // tm,),
        in_specs=[pl.BlockSpec((tm, d), lambda i: (i, 0)),
                  pl.BlockSpec((d, cols), lambda i: (0, 0))],
        out_specs=[pl.BlockSpec((tm, ssm_w), lambda i: (i, 0)),
                   pl.BlockSpec((tm, pool_w), lambda i: (i, 0)),
                   pl.BlockSpec((tm, gate_w), lambda i: (i, 0))],
        out_shape=[jax.ShapeDtypeStruct((t, ssm_w), F32),
                   jax.ShapeDtypeStruct((t, pool_w), F32),
                   jax.ShapeDtypeStruct((t, gate_w), BF16)],
        compiler_params=_params("parallel"),
        name="mix_in",
    )(x, w_in)


def _mix_mid_kernel(us_ref, up_ref, bin_ref, cout_ref, pos_re_ref, pos_im_ref, neg_re_ref, neg_im_ref,
                    tri_ref, d_ref, wglu_ref, bglu_ref, pw_ref, ps_ref,
                    ys_ref, yp_ref, carry_ref, w_ref, h_ref, halo_ref):
    ti = pl.program_id(1)
    ts = us_ref.shape[0]
    n_blocks = bin_ref.shape[0]
    half = bin_ref.shape[2] // 2
    ch = bin_ref.shape[1]

    @pl.when(ti == 0)
    def _():
        carry_ref[...] = jnp.zeros_like(carry_ref)
        halo_ref[...] = jnp.zeros_like(halo_ref)

    u = us_ref[...]
    ub = u.astype(BF16)
    ys = []
    for j in range(n_blocks):
        bu = _bdot(ub[:, j * ch:(j + 1) * ch], bin_ref[j])
        bu_re = bu[:, :half].reshape(ts // SCAN_ROWS, SCAN_ROWS, half)
        bu_im = bu[:, half:].reshape(ts // SCAN_ROWS, SCAN_ROWS, half)
        n_re = neg_re_ref[:, j * half:(j + 1) * half][None]
        n_im = neg_im_ref[:, j * half:(j + 1) * half][None]
        z_re = (n_re * bu_re - n_im * bu_im).reshape(ts, half)
        z_im = (n_re * bu_im + n_im * bu_re).reshape(ts, half)
        z = jnp.concatenate([z_re, z_im], axis=1).astype(BF16)
        w_ref[...] = _bdot(tri_ref[...], z)
        p_re = pos_re_ref[:, j * half:(j + 1) * half]
        p_im = pos_im_ref[:, j * half:(j + 1) * half]
        c0 = j * 2 * half

        def block(i, carry, p_re=p_re, p_im=p_im, c0=c0):
            r0 = pl.multiple_of(i * SCAN_ROWS, SCAN_ROWS)
            s_re = w_ref[pl.ds(r0, SCAN_ROWS), :half] + carry_ref[:, c0:c0 + half]
            s_im = w_ref[pl.ds(r0, SCAN_ROWS), half:] + carry_ref[:, c0 + half:c0 + 2 * half]
            h_re = p_re * s_re - p_im * s_im
            h_im = p_re * s_im + p_im * s_re
            h_ref[pl.ds(r0, SCAN_ROWS), :half] = h_re
            h_ref[pl.ds(r0, SCAN_ROWS), half:] = h_im
            carry_ref[:, c0:c0 + half] = jnp.broadcast_to(h_re[SCAN_ROWS - 1:, :], (SCAN_ROWS, half))
            carry_ref[:, c0 + half:c0 + 2 * half] = jnp.broadcast_to(h_im[SCAN_ROWS - 1:, :], (SCAN_ROWS, half))
            return carry

        lax.fori_loop(0, ts // SCAN_ROWS, block, 0)
        ys.append(_bdot(h_ref[...].astype(BF16), cout_ref[j]))
    y = jnp.concatenate(ys, axis=1) + d_ref[...] * u
    y = jax.nn.gelu(y)
    y = y * _sigmoid(_bdot(y.astype(BF16), wglu_ref[...]) + bglu_ref[...])
    ys_ref[...] = y.astype(ys_ref.dtype)

    up = up_ref[...]
    ext = jnp.concatenate([halo_ref[...], up], axis=0)
    halo_ref[...] = up[ts - POOL_HALO:, :]
    pg = pw_ref.shape[1]
    t_pos = (ti * ts + lax.broadcasted_iota(jnp.int32, (ts, 1), 0) + 1).astype(F32)
    outs = []
    for gi, win in enumerate(POOL_WINDOWS):
        e = ext[:, gi * pg:(gi + 1) * pg]
        s = e
        sh = 1
        while sh < win:
            s = s + pltpu.roll(s, sh, 0)
            sh *= 2
        mean = s[POOL_HALO:, :] / jnp.minimum(t_pos, float(win))
        dlt = (mean - e[POOL_HALO:, :]).astype(BF16)
        outs.append(_bdot(dlt, pw_ref[gi]))
    yp = jnp.concatenate(outs, axis=1) * ps_ref[...]
    yp_ref[...] = yp.astype(yp_ref.dtype)


def _mix_mid(us, up, batch, bin_blk, cout_blk, pos_re, pos_im, neg_re, neg_im, tri, ssm_d, w_glu, b_glu,
             pool_w, pool_scale):
    t, sw = us.shape
    pw = up.shape[1]
    seq = t // batch
    ts = TS_SCAN
    nt = seq // ts
    nb, ch, st2 = bin_blk.shape
    n_state = pos_re.shape[1]

    def full(a):
        return pl.BlockSpec(a.shape, lambda b, i, n=a.ndim: (0,) * n)

    row = lambda b, i: (b * nt + i, 0)
    return pl.pallas_call(
        _mix_mid_kernel,
        grid=(batch, nt),
        in_specs=[pl.BlockSpec((ts, sw), row), pl.BlockSpec((ts, pw), row),
                  full(bin_blk), full(cout_blk), full(pos_re), full(pos_im), full(neg_re), full(neg_im),
                  full(tri), full(ssm_d), full(w_glu), full(b_glu), full(pool_w), full(pool_scale)],
        out_specs=[pl.BlockSpec((ts, sw), row), pl.BlockSpec((ts, pw), row)],
        out_shape=[jax.ShapeDtypeStruct((t, sw), BF16), jax.ShapeDtypeStruct((t, pw), BF16)],
        scratch_shapes=[pltpu.VMEM((SCAN_ROWS, 2 * n_state), F32),
                        pltpu.VMEM((ts, st2), F32),
                        pltpu.VMEM((ts, st2), F32),
                        pltpu.VMEM((POOL_HALO, pw), F32)],
        compiler_params=_params("parallel", "arbitrary"),
        name="mix_mid",
    )(us, up, bin_blk, cout_blk, pos_re, pos_im, neg_re, neg_im, tri, ssm_d, w_glu, b_glu, pool_w, pool_scale)


def _mix_out_kernel(x_ref, ys_ref, yp_ref, g_ref, wsu_ref, wpu_ref, wo_ref, lg_ref, lb_ref, o_ref, *, alpha):
    d = x_ref.shape[1]
    y_ssm = _bdot(ys_ref[...], wsu_ref[...])
    y_pool = _bdot(yp_ref[...], wpu_ref[...])
    g = g_ref[...].astype(F32)
    comb = g[:, :d] * y_ssm + g[:, d:] * y_pool
    y = _bdot(comb.astype(BF16), wo_ref[...])
    o_ref[...] = _layer_norm(alpha * x_ref[...] + y, lg_ref[...], lb_ref[...])


def _mix_out(x, ys, yp, gates, w_ssm_up, w_pool_up, w_out, ln_g, ln_b, alpha):
    t, d = x.shape
    tm = TM_PROJ
    row = lambda i: (i, 0)
    full = lambda a: pl.BlockSpec(a.shape, lambda i, n=a.ndim: (0,) * n)
    return pl.pallas_call(
        functools.partial(_mix_out_kernel, alpha=alpha),
        grid=(t // tm,),
        in_specs=[pl.BlockSpec((tm, d), row), pl.BlockSpec((tm, ys.shape[1]), row),
                  pl.BlockSpec((tm, yp.shape[1]), row), pl.BlockSpec((tm, gates.shape[1]), row),
                  full(w_ssm_up), full(w_pool_up), full(w_out), full(ln_g), full(ln_b)],
        out_specs=pl.BlockSpec((tm, d), row),
        out_shape=jax.ShapeDtypeStruct((t, d), F32),
        compiler_params=_params("parallel"),
        name="mix_out",
    )(x, ys, yp, gates, w_ssm_up, w_pool_up, w_out, ln_g, ln_b)


def _kv_kernel(m_ref, wk_ref, wv_ref, k_ref, v_ref):
    mb = m_ref[...].astype(BF16)
    k_ref[...] = _bdot(mb, wk_ref[...]).astype(BF16)
    v_ref[...] = _bdot(mb, wv_ref[...]).astype(BF16)


def _kv_proj(mem2d, wk, wv):
    r, d = mem2d.shape
    tm = 256
    full = lambda a: pl.BlockSpec(a.shape, lambda i: (0, 0))
    return pl.pallas_call(
        _kv_kernel,
        grid=(r // tm,),
        in_specs=[pl.BlockSpec((tm, d), lambda i: (i, 0)), full(wk), full(wv)],
        out_specs=[pl.BlockSpec((tm, d), lambda i: (i, 0))] * 2,
        out_shape=[jax.ShapeDtypeStruct((r, d), BF16)] * 2,
        compiler_params=_params("parallel"),
        name="xattn_kv",
    )(mem2d, wk, wv)


def _xattn_kernel(x_ref, k_ref, v_ref, wq_ref, wo_ref, lg_ref, lb_ref, o_ref, *, alpha):
    x = x_ref[...]
    d = x.shape[1]
    hd = d // N_XHEADS
    q = _bdot(x.astype(BF16), wq_ref[...]) * (hd ** -0.5)
    heads = []
    for h in range(N_XHEADS):
        qh = q[:, h * hd:(h + 1) * hd].astype(BF16)
        kh = k_ref[:, h * hd:(h + 1) * hd]
        s = lax.dot_general(qh, kh, (((1,), (1,)), ((), ())), preferred_element_type=F32)
        s = s - jnp.max(s, axis=-1, keepdims=True)
        p = jnp.exp(s)
        p = p / jnp.sum(p, axis=-1, keepdims=True)
        heads.append(_bdot(p.astype(BF16), v_ref[:, h * hd:(h + 1) * hd]))
    att = jnp.concatenate(heads, axis=1).astype(BF16)
    y = _bdot(att, wo_ref[...])
    o_ref[...] = _layer_norm(alpha * x + y, lg_ref[...], lb_ref[...])


def _xattn(x, k, v, batch, wq, wo, ln_g, ln_b, alpha):
    t, d = x.shape
    n_mem = k.shape[0] // batch
    tm = TM_ATTN
    nt = t // batch // tm
    full = lambda a: pl.BlockSpec(a.shape, lambda b, i, n=a.ndim: (0,) * n)
    row = lambda b, i: (b * nt + i, 0)
    return pl.pallas_call(
        functools.partial(_xattn_kernel, alpha=alpha),
        grid=(batch, nt),
        in_specs=[pl.BlockSpec((tm, d), row),
                  pl.BlockSpec((n_mem, d), lambda b, i: (b, 0)), pl.BlockSpec((n_mem, d), lambda b, i: (b, 0)),
                  full(wq), full(wo), full(ln_g), full(ln_b)],
        out_specs=pl.BlockSpec((tm, d), row),
        out_shape=jax.ShapeDtypeStruct((t, d), F32),
        compiler_params=_params("parallel", "parallel"),
        name="xattn",
    )(x, k, v, wq, wo, ln_g, ln_b)


def _ffn_kernel(x_ref, w1_ref, w3_ref, w2_ref, lg_ref, lb_ref, o_ref, acc_ref, *, alpha):
    f = pl.program_id(1)
    xb = x_ref[...].astype(BF16)
    a = _bdot(xb, w1_ref[...])
    h = (a * _sigmoid(a) * _bdot(xb, w3_ref[...])).astype(BF16)
    part = _bdot(h, w2_ref[...])

    @pl.when(f == 0)
    def _():
        acc_ref[...] = part

    @pl.when(f > 0)
    def _():
        acc_ref[...] += part

    @pl.when(f == pl.num_programs(1) - 1)
    def _():
        o_ref[...] = _layer_norm(alpha * x_ref[...] + acc_ref[...], lg_ref[...], lb_ref[...])


def _ffn_tile(ff):
    for cand in (512, 1408, 1024, 768, 256, 128):
        if ff % cand == 0:
            return cand
    return ff


def _ffn(x, w1, w3, w2, ln_g, ln_b, alpha):
    t, d = x.shape
    ff = w1.shape[1]
    tm, tf = TM_FFN, _ffn_tile(ff)
    full = lambda a: pl.BlockSpec(a.shape, lambda i, f, n=a.ndim: (0,) * n)
    return pl.pallas_call(
        functools.partial(_ffn_kernel, alpha=alpha),
        grid=(t // tm, ff // tf),
        in_specs=[pl.BlockSpec((tm, d), lambda i, f: (i, 0)),
                  pl.BlockSpec((d, tf), lambda i, f: (0, f)), pl.BlockSpec((d, tf), lambda i, f: (0, f)),
                  pl.BlockSpec((tf, d), lambda i, f: (f, 0)), full(ln_g), full(ln_b)],
        out_specs=pl.BlockSpec((tm, d), lambda i, f: (i, 0)),
        out_shape=jax.ShapeDtypeStruct((t, d), F32),
        scratch_shapes=[pltpu.VMEM((tm, d), F32)],
        compiler_params=_params("parallel", "arbitrary"),
        name="ffn_dense",
    )(x, w1, w3, w2, ln_g, ln_b)


def _router_kernel(x_ref, r_ref, o_ref):
    logits = jnp.dot(x_ref[...], r_ref[...], preferred_element_type=F32, precision=lax.Precision.HIGHEST)
    lane = lax.broadcasted_iota(jnp.int32, logits.shape, 1)
    lane_f = lane.astype(F32)
    neg = jnp.float32(-jnp.inf)
    lg = jnp.where(lane < N_EXPERTS, logits, neg)
    m1 = jnp.max(lg, axis=-1, keepdims=True)
    i1 = jnp.min(jnp.where(lg == m1, lane_f, float(LANES)), axis=-1, keepdims=True)
    lg2 = jnp.where(lane_f == i1, neg, lg)
    m2 = jnp.max(lg2, axis=-1, keepdims=True)
    i2 = jnp.min(jnp.where(lg2 == m2, lane_f, float(LANES)), axis=-1, keepdims=True)
    e = jnp.exp(m2 - m1)
    w1 = 1.0 / (1.0 + e)
    w2 = e / (1.0 + e)
    out = jnp.where(lane == 0, w1, jnp.where(lane == 1, w2, jnp.where(lane == 2, i1, jnp.where(lane == 3, i2, 0.0))))
    o_ref[...] = out


def _router(x, router_pad):
    t, d = x.shape
    tm = TM_ROUTER
    return pl.pallas_call(
        _router_kernel,
        grid=(t // tm,),
        in_specs=[pl.BlockSpec((tm, d), lambda i: (i, 0)), pl.BlockSpec(router_pad.shape, lambda i: (0, 0))],
        out_specs=pl.BlockSpec((tm, LANES), lambda i: (i, 0)),
        out_shape=jax.ShapeDtypeStruct((t, LANES), F32),
        compiler_params=_params("parallel"),
        name="moe_router",
    )(x, router_pad)


def _gather_rows_kernel(idx_hbm, x_hbm, o_hbm, idx_smem, idx_sem, row_sem):
    i = pl.program_id(0)
    ch = idx_smem.shape[0]
    cp = pltpu.make_async_copy(idx_hbm.at[i], idx_smem, idx_sem)
    cp.start()
    cp.wait()

    def issue(r, c):
        pltpu.make_async_copy(x_hbm.at[pl.ds(idx_smem[r], 1)], o_hbm.at[pl.ds(i * ch + r, 1)], row_sem).start()
        return c

    lax.fori_loop(0, ch, issue, 0)
    pltpu.make_async_copy(x_hbm.at[pl.ds(0, ch)], o_hbm.at[pl.ds(i * ch, ch)], row_sem).wait()


def _gather_rows(x, idx2d):
    n_chunks, ch = idx2d.shape
    d = x.shape[1]
    return pl.pallas_call(
        _gather_rows_kernel,
        grid=(n_chunks,),
        in_specs=[pl.BlockSpec(memory_space=pl.ANY), pl.BlockSpec(memory_space=pl.ANY)],
        out_specs=pl.BlockSpec(memory_space=pl.ANY),
        out_shape=jax.ShapeDtypeStruct((n_chunks * ch, d), x.dtype),
        scratch_shapes=[pltpu.SMEM((ch,), jnp.int32), pltpu.SemaphoreType.DMA, pltpu.SemaphoreType.DMA],
        compiler_params=_params("arbitrary"),
        name="moe_gather",
    )(idx2d, x)


def _expert_kernel(te_ref, tv_ref, x_ref, w1_ref, w3_ref, w2_ref, o_ref, acc_ref):
    i = pl.program_id(0)
    f = pl.program_id(1)
    valid = tv_ref[i] == 1

    @pl.when(valid)
    def _():
        xb = x_ref[...].astype(BF16)
        a = _bdot(xb, w1_ref[...])
        h = (a * _sigmoid(a) * _bdot(xb, w3_ref[...])).astype(BF16)
        part = _bdot(h, w2_ref[...])

        @pl.when(f == 0)
        def _():
            acc_ref[...] = part

        @pl.when(f > 0)
        def _():
            acc_ref[...] += part

    @pl.when(f == pl.num_programs(1) - 1)
    def _():
        @pl.when(valid)
        def _():
            o_ref[...] = acc_ref[...]

        @pl.when(jnp.logical_not(valid))
        def _():
            o_ref[...] = jnp.zeros_like(o_ref)


def _experts(xs, tile_expert, tile_valid, w1, w3, w2):
    r, d = xs.shape
    ff = w1.shape[2]
    tm, tf = TM_EXPERT, _ffn_tile(ff)
    grid_spec = pltpu.PrefetchScalarGridSpec(
        num_scalar_prefetch=2,
        grid=(r // tm, ff // tf),
        in_specs=[pl.BlockSpec((tm, d), lambda i, f, te, tv: (i, 0)),
                  pl.BlockSpec((None, d, tf), lambda i, f, te, tv: (te[i], 0, f)),
                  pl.BlockSpec((None, d, tf), lambda i, f, te, tv: (te[i], 0, f)),
                  pl.BlockSpec((None, tf, d), lambda i, f, te, tv: (te[i], f, 0))],
        out_specs=pl.BlockSpec((tm, d), lambda i, f, te, tv: (i, 0)),
        scratch_shapes=[pltpu.VMEM((tm, d), F32)],
    )
    return pl.pallas_call(
        _expert_kernel,
        grid_spec=grid_spec,
        out_shape=jax.ShapeDtypeStruct((r, d), F32),
        compiler_params=_params("arbitrary", "arbitrary"),
        name="moe_experts",
    )(tile_expert, tile_valid, xs, w1, w3, w2)


def _combine_kernel(pos_hbm, ys_hbm, info_ref, x_ref, lg_ref, lb_ref, o_ref, pos_smem, buf_ref, idx_sem, row_sem,
                    *, alpha):
    i = pl.program_id(0)
    tm = x_ref.shape[0]
    cp = pltpu.make_async_copy(pos_hbm.at[i], pos_smem, idx_sem)
    cp.start()
    cp.wait()

    def issue(r, c):
        pltpu.make_async_copy(ys_hbm.at[pl.ds(pos_smem[2 * r], 1)], buf_ref.at[0, pl.ds(r, 1)], row_sem).start()
        pltpu.make_async_copy(ys_hbm.at[pl.ds(pos_smem[2 * r + 1], 1)], buf_ref.at[1, pl.ds(r, 1)], row_sem).start()
        return c

    lax.fori_loop(0, tm, issue, 0)
    for k in range(2):
        pltpu.make_async_copy(ys_hbm.at[pl.ds(0, tm)], buf_ref.at[k], row_sem).wait()
    info = info_ref[...]
    y = info[:, 0:1] * buf_ref[0] + info[:, 1:2] * buf_ref[1]
    o_ref[...] = _layer_norm(alpha * x_ref[...] + y, lg_ref[...], lb_ref[...])


def _combine(pos2d, ys, info, x, ln_g, ln_b, alpha):
    t, d = x.shape
    tm = TM_COMBINE
    full = lambda a: pl.BlockSpec(a.shape, lambda i, n=a.ndim: (0,) * n)
    return pl.pallas_call(
        functools.partial(_combine_kernel, alpha=alpha),
        grid=(t // tm,),
        in_specs=[pl.BlockSpec(memory_space=pl.ANY), pl.BlockSpec(memory_space=pl.ANY),
                  pl.BlockSpec((tm, LANES), lambda i: (i, 0)), pl.BlockSpec((tm, d), lambda i: (i, 0)),
                  full(ln_g), full(ln_b)],
        out_specs=pl.BlockSpec((tm, d), lambda i: (i, 0)),
        out_shape=jax.ShapeDtypeStruct((t, d), F32),
        scratch_shapes=[pltpu.SMEM((2 * tm,), jnp.int32), pltpu.VMEM((2, tm, d), F32),
                        pltpu.SemaphoreType.DMA, pltpu.SemaphoreType.DMA],
        compiler_params=_params("arbitrary"),
        name="moe_combine",
    )(pos2d, ys, info, x, ln_g, ln_b)


def _moe(x, router, w1, w3, w2, ln_g, ln_b, alpha):
    t, d = x.shape
    ne = router.shape[1]
    router_pad = jnp.pad(router, ((0, 0), (0, LANES - ne)))
    info = _router(x, router_pad)
    top_idx = info[:, 2:4].astype(jnp.int32).reshape(-1)
    onehot = (top_idx[:, None] == jnp.arange(ne, dtype=jnp.int32)[None, :]).astype(jnp.int32)
    rank = jnp.sum((jnp.cumsum(onehot, axis=0) - onehot) * onehot, axis=1)
    counts = jnp.sum(onehot, axis=0)
    padded = ((counts + TM_EXPERT - 1) // TM_EXPERT) * TM_EXPERT
    ends = jnp.cumsum(padded)
    offs = ends - padded
    pos = offs[top_idx] + rank
    n_rows = 2 * t + ne * TM_EXPERT
    n_rows = ((n_rows + GATHER_CHUNK - 1) // GATHER_CHUNK) * GATHER_CHUNK
    src = jnp.zeros((n_rows,), jnp.int32).at[pos].set(jnp.arange(2 * t, dtype=jnp.int32) // 2)
    tile_start = jnp.arange(n_rows // TM_EXPERT, dtype=jnp.int32) * TM_EXPERT
    tile_valid = (tile_start < ends[-1]).astype(jnp.int32)
    tile_expert = jnp.minimum(jnp.searchsorted(ends, tile_start, side="right"), ne - 1).astype(jnp.int32)
    tile_expert = jnp.where(tile_valid == 1, tile_expert, tile_expert[jnp.maximum(ends[-1] // TM_EXPERT - 1, 0)])

    xs = _gather_rows(x, src.reshape(-1, GATHER_CHUNK))
    ys = _experts(xs, tile_expert, tile_valid, w1, w3, w2)
    return _combine(pos.reshape(-1, 2 * TM_COMBINE), ys, info, x, ln_g, ln_b, alpha)


def kernel(x, mem, ln_mix_g, ln_mix_b, w_in, ssm_a_re, ssm_a_im, ssm_log_dt, ssm_b_re, ssm_b_im, ssm_c_re, ssm_c_im, ssm_d, ssm_w_glu, ssm_b_glu, w_ssm_up, pool_w, pool_scale, w_pool_up, w_out, ln_xa_g, ln_xa_b, xa_wq, xa_wk, xa_wv, xa_wo, ln_ffn_g, ln_ffn_b, ffn_w1, ffn_w3, ffn_w2, moe_router, moe_w1, moe_w3, moe_w2):
    batch, seq, d = x.shape
    depth = w_in.shape[0]
    alpha = (2 * depth) ** 0.25
    ssm_w = ssm_d.shape[1]
    pool_width = pool_scale.shape[1]
    t = batch * seq
    assert seq % TS_SCAN == 0 and t % TM_PROJ == 0 and TS_SCAN % SCAN_ROWS == 0

    pos_re, pos_im, neg_re, neg_im, bb_re, bb_im = _ssm_prep(ssm_a_re, ssm_a_im, ssm_log_dt, ssm_b_re, ssm_b_im)
    r = jnp.arange(TS_SCAN)
    tri = ((r[:, None] // SCAN_ROWS == r[None, :] // SCAN_ROWS) & (r[None, :] <= r[:, None])).astype(BF16)
    bf = lambda a: a.astype(BF16)
    row = lambda a: a.reshape(1, -1)

    xf = x.reshape(t, d)
    mem2d = mem.reshape(-1, d)
    for l in range(depth):
        us, up, gates = _mix_in(xf, bf(w_in[l]), ssm_w, pool_width)
        ys, yp = _mix_mid(us, up, batch, _block_diag_in(bb_re[l], bb_im[l]),
                          _block_diag_out(ssm_c_re[l], ssm_c_im[l]),
                          pos_re[l], pos_im[l], neg_re[l], neg_im[l], tri, row(ssm_d[l]), bf(ssm_w_glu[l]),
                          row(ssm_b_glu[l]), bf(pool_w[l]), row(pool_scale[l]))
        xf = _mix_out(xf, ys, yp, gates, bf(w_ssm_up[l]), bf(w_pool_up[l]), bf(w_out[l]),
                      row(ln_mix_g[l]), row(ln_mix_b[l]), alpha)
        k, v = _kv_proj(mem2d, bf(xa_wk[l]), bf(xa_wv[l]))
        xf = _xattn(xf, k, v, batch, bf(xa_wq[l]), bf(xa_wo[l]), row(ln_xa_g[l]), row(ln_xa_b[l]), alpha)
        if l % 2 == 0:
            i = l // 2
            xf = _ffn(xf, bf(ffn_w1[i]), bf(ffn_w3[i]), bf(ffn_w2[i]), row(ln_ffn_g[l]), row(ln_ffn_b[l]), alpha)
        else:
            i = l // 2
            xf = _moe(xf, moe_router[i], bf(moe_w1[i]), bf(moe_w3[i]), bf(moe_w2[i]),
                      row(ln_ffn_g[l]), row(ln_ffn_b[l]), alpha)
    return xf.reshape(batch, seq, d)
```

```python
import functools
import math

import jax
import jax.numpy as jnp
from jax import lax
from jax.experimental import pallas as pl
from jax.experimental.pallas import tpu as pltpu

F32 = jnp.float32
BF16 = jnp.bfloat16

LN_EPS = 1e-5
SSM_GROUP = 16
SSM_STATE = 64
GROUPS_PER_BLOCK = 8
SCAN_ROWS = 8
POOL_WINDOWS = (2, 4, 8, 16)
POOL_HALO = 16
N_XHEADS = 4
N_EXPERTS = 8
LANES = 128

VMEM_LIMIT_BYTES = 56 * 1024 * 1024

TM_PROJ = 512
TS_SCAN = 256
TM_ATTN = 512
TM_FFN = 512
TM_ROUTER = 512
TM_EXPERT = 1024
TM_COMBINE = 256
GATHER_CHUNK = 1024


def _params(*sem):
    return pltpu.CompilerParams(dimension_semantics=sem, vmem_limit_bytes=VMEM_LIMIT_BYTES)


def _layer_norm(z, g, b):
    mu = jnp.mean(z, axis=-1, keepdims=True)
    zc = z - mu
    var = jnp.mean(zc * zc, axis=-1, keepdims=True)
    return zc * lax.rsqrt(var + LN_EPS) * g + b


def _sigmoid(v):
    return 1.0 / (1.0 + jnp.exp(-v))


def _bdot(a, b):
    return jnp.dot(a, b, preferred_element_type=F32)


def _ssm_prep_kernel(a_re_ref, a_im_ref, log_dt_ref, b_re_ref, b_im_ref,
                     pos_re_ref, pos_im_ref, neg_re_ref, neg_im_ref, bb_re_ref, bb_im_ref):
    a_re = a_re_ref[...]
    a_im = a_im_ref[...]
    dt = jnp.exp(log_dt_ref[...])
    mag = jnp.exp(a_re * dt)
    lam_re = mag * jnp.cos(a_im * dt)
    lam_im = mag * jnp.sin(a_im * dt)
    den = a_re * a_re + a_im * a_im
    num_re = lam_re - 1.0
    f_re = (num_re * a_re + lam_im * a_im) / den
    f_im = (lam_im * a_re - num_re * a_im) / den
    b_re = b_re_ref[...]
    b_im = b_im_ref[...]
    bb_re_ref[...] = f_re[:, None, :] * b_re - f_im[:, None, :] * b_im
    bb_im_ref[...] = f_re[:, None, :] * b_im + f_im[:, None, :] * b_re
    m2 = lam_re * lam_re + lam_im * lam_im
    inv_re = lam_re / m2
    inv_im = -lam_im / m2
    p_re, p_im = lam_re, lam_im
    n_re, n_im = inv_re, inv_im
    for k in range(SCAN_ROWS):
        pos_re_ref[k] = p_re
        pos_im_ref[k] = p_im
        neg_re_ref[k] = n_re
        neg_im_ref[k] = n_im
        p_re, p_im = p_re * lam_re - p_im * lam_im, p_re * lam_im + p_im * lam_re
        n_re, n_im = n_re * inv_re - n_im * inv_im, n_re * inv_im + n_im * inv_re


def _ssm_prep(a_re, a_im, log_dt, b_re, b_im):
    nl, g, p = a_re.shape
    c = b_re.shape[-1]
    lg = nl * g
    tab = jax.ShapeDtypeStruct((SCAN_ROWS, lg, p), F32)
    bbs = jax.ShapeDtypeStruct((lg, c, p), F32)
    outs = pl.pallas_call(
        _ssm_prep_kernel,
        out_shape=(tab, tab, tab, tab, bbs, bbs),
        name="ssm_prep",
    )(a_re.reshape(lg, p), a_im.reshape(lg, p), log_dt.reshape(lg, 1),
      b_re.transpose(0, 1, 3, 2).reshape(lg, c, p), b_im.transpose(0, 1, 3, 2).reshape(lg, c, p))
    pos_re, pos_im, neg_re, neg_im, bb_re, bb_im = outs

    def tab_layout(t):
        return t.reshape(SCAN_ROWS, nl, g * p).transpose(1, 0, 2)

    return (tab_layout(pos_re), tab_layout(pos_im), tab_layout(neg_re), tab_layout(neg_im),
            bb_re.reshape(nl, g, c, p), bb_im.reshape(nl, g, c, p))


def _block_diag_in(bb_re, bb_im):
    g, c, p = bb_re.shape
    nb = g // GROUPS_PER_BLOCK
    eye = jnp.eye(GROUPS_PER_BLOCK, dtype=F32)

    def one(bb):
        bb = bb.reshape(nb, GROUPS_PER_BLOCK, c, p)
        m = bb[:, :, :, None, :] * eye[None, :, None, :, None]
        return m.reshape(nb, GROUPS_PER_BLOCK * c, GROUPS_PER_BLOCK * p)

    return jnp.concatenate([one(bb_re), one(bb_im)], axis=-1).astype(BF16)


def _block_diag_out(c_re, c_im):
    g, c, p = c_re.shape
    nb = g // GROUPS_PER_BLOCK
    eye = jnp.eye(GROUPS_PER_BLOCK, dtype=F32)

    def one(cm):
        cm = cm.reshape(nb, GROUPS_PER_BLOCK, c, p).transpose(0, 1, 3, 2)
        m = cm[:, :, :, None, :] * eye[None, :, None, :, None]
        return m.reshape(nb, GROUPS_PER_BLOCK * p, GROUPS_PER_BLOCK * c)

    return jnp.concatenate([one(c_re), -one(c_im)], axis=1).astype(BF16)


def _mix_in_kernel(x_ref, w_ref, us_ref, up_ref, g_ref, *, ssm_w, pool_w):
    xb = x_ref[...].astype(BF16)
    us_ref[...] = _bdot(xb, w_ref[:, :ssm_w])
    up_ref[...] = _bdot(xb, w_ref[:, ssm_w:ssm_w + pool_w])
    g_ref[...] = _sigmoid(_bdot(xb, w_ref[:, ssm_w + pool_w:])).astype(BF16)


def _mix_in(x, w_in, ssm_w, pool_w):
    t, d = x.shape
    cols = w_in.shape[1]
    gate_w = cols - ssm_w - pool_w
    tm = TM_PROJ
    return pl.pallas_call(
        functools.partial(_mix_in_kernel, ssm_w=ssm_w, pool_w=pool_w),
        grid=(t // tm,),
        in_specs=[pl.BlockSpec((tm, d), lambda i: (i, 0)),
                  pl.BlockSpec((d, cols), lambda i: (0, 0))],
        out_specs=[pl.BlockSpec((tm, ssm_w), lambda i: (i, 0)),
                   pl.BlockSpec((tm, pool_w), lambda i: (i, 0)),
                   pl.BlockSpec((tm, gate_w), lambda i: (i, 0))],
        out_shape=[jax.ShapeDtypeStruct((t, ssm_w), F32),
                   jax.ShapeDtypeStruct((t, pool_w), F32),
                   jax.ShapeDtypeStruct((t, gate_w), BF16)],
        compiler_params=_params("parallel"),
        name="mix_in",
    )(x, w_in)


def _mix_mid_kernel(us_ref, up_ref, bin_ref, cout_ref, pos_re_ref, pos_im_ref, neg_re_ref, neg_im_ref,
                    tri_ref, d_ref, wglu_ref, bglu_ref, pw_ref, ps_ref,
                    ys_ref, yp_ref, carry_ref, w_ref, h_ref, halo_ref):
    ti = pl.program_id(1)
    ts = us_ref.shape[0]
    n_blocks = bin_ref.shape[0]
    half = bin_ref.shape[2] // 2
    ch = bin_ref.shape[1]

    @pl.when(ti == 0)
    def _():
        carry_ref[...] = jnp.zeros_like(carry_ref)
        halo_ref[...] = jnp.zeros_like(halo_ref)

    u = us_ref[...]
    ub = u.astype(BF16)
    ys = []
    for j in range(n_blocks):
        bu = _bdot(ub[:, j * ch:(j + 1) * ch], bin_ref[j])
        bu_re = bu[:, :half].reshape(ts // SCAN_ROWS, SCAN_ROWS, half)
        bu_im = bu[:, half:].reshape(ts // SCAN_ROWS, SCAN_ROWS, half)
        n_re = neg_re_ref[:, j * half:(j + 1) * half][None]
        n_im = neg_im_ref[:, j * half:(j + 1) * half][None]
        z_re = (n_re * bu_re - n_im * bu_im).reshape(ts, half)
        z_im = (n_re * bu_im + n_im * bu_re).reshape(ts, half)
        z = jnp.concatenate([z_re, z_im], axis=1).astype(BF16)
        w_ref[...] = _bdot(tri_ref[...], z)
        p_re = pos_re_ref[:, j * half:(j + 1) * half]
        p_im = pos_im_ref[:, j * half:(j + 1) * half]
        c0 = j * 2 * half

        def block(i, carry, p_re=p_re, p_im=p_im, c0=c0):
            r0 = pl.multiple_of(i * SCAN_ROWS, SCAN_ROWS)
            s_re = w_ref[pl.ds(r0, SCAN_ROWS), :half] + carry_ref[:, c0:c0 + half]
            s_im = w_ref[pl.ds(r0, SCAN_ROWS), half:] + carry_ref[:, c0 + half:c0 + 2 * half]
            h_re = p_re * s_re - p_im * s_im
            h_im = p_re * s_im + p_im * s_re
            h_ref[pl.ds(r0, SCAN_ROWS), :half] = h_re
            h_ref[pl.ds(r0, SCAN_ROWS), half:] = h_im
            carry_ref[:, c0:c0 + half] = jnp.broadcast_to(h_re[SCAN_ROWS - 1:, :], (SCAN_ROWS, half))
            carry_ref[:, c0 + half:c0 + 2 * half] = jnp.broadcast_to(h_im[SCAN_ROWS - 1:, :], (SCAN_ROWS, half))
            return carry

        lax.fori_loop(0, ts // SCAN_ROWS, block, 0)
        ys.append(_bdot(h_ref[...].astype(BF16), cout_ref[j]))
    y = jnp.concatenate(ys, axis=1) + d_ref[...] * u
    y = jax.nn.gelu(y)
    y = y * _sigmoid(_bdot(y.astype(BF16), wglu_ref[...]) + bglu_ref[...])
    ys_ref[...] = y.astype(ys_ref.dtype)

    up = up_ref[...]
    ext = jnp.concatenate([halo_ref[...], up], axis=0)
    halo_ref[...] = up[ts - POOL_HALO:, :]
    pg = pw_ref.shape[1]
    t_pos = (ti * ts + lax.broadcasted_iota(jnp.int32, (ts, 1), 0) + 1).astype(F32)
    outs = []
    for gi, win in enumerate(POOL_WINDOWS):
        e = ext[:, gi * pg:(gi + 1) * pg]
        s = e
        sh = 1
        while sh < win:
            s = s + pltpu.roll(s, sh, 0)
            sh *= 2
        mean = s[POOL_HALO:, :] / jnp.minimum(t_pos, float(win))
        dlt = (mean - e[POOL_HALO:, :]).astype(BF16)
        outs.append(_bdot(dlt, pw_ref[gi]))
    yp = jnp.concatenate(outs, axis=1) * ps_ref[...]
    yp_ref[...] = yp.astype(yp_ref.dtype)


def _mix_mid(us, up, batch, bin_blk, cout_blk, pos_re, pos_im, neg_re, neg_im, tri, ssm_d, w_glu, b_glu,
             pool_w, pool_scale):
    t, sw = us.shape
    pw = up.shape[1]
    seq = t // batch
    ts = TS_SCAN
    nt = seq // ts
    nb, ch, st2 = bin_blk.shape
    n_state = pos_re.shape[1]

    def full(a):
        return pl.BlockSpec(a.shape, lambda b, i, n=a.ndim: (0,) * n)

    row = lambda b, i: (b * nt + i, 0)
    return pl.pallas_call(
        _mix_mid_kernel,
        grid=(batch, nt),
        in_specs=[pl.BlockSpec((ts, sw), row), pl.BlockSpec((ts, pw), row),
                  full(bin_blk), full(cout_blk), full(pos_re), full(pos_im), full(neg_re), full(neg_im),
                  full(tri), full(ssm_d), full(w_glu), full(b_glu), full(pool_w), full(pool_scale)],
        out_specs=[pl.BlockSpec((ts, sw), row), pl.BlockSpec((ts, pw), row)],
        out_shape=[jax.ShapeDtypeStruct((t, sw), BF16), jax.ShapeDtypeStruct((t, pw), BF16)],
        scratch_shapes=[pltpu.VMEM((SCAN_ROWS, 2 * n_state), F32),
                        pltpu.VMEM((ts, st2), F32),
                        pltpu.VMEM((ts, st2), F32),
                        pltpu.VMEM((POOL_HALO, pw), F32)],
        compiler_params=_params("parallel", "arbitrary"),
        name="mix_mid",
    )(us, up, bin_blk, cout_blk, pos_re, pos_im, neg_re, neg_im, tri, ssm_d, w_glu, b_glu, pool_w, pool_scale)


def _mix_out_kernel(x_ref, ys_ref, yp_ref, g_ref, wsu_ref, wpu_ref, wo_ref, lg_ref, lb_ref, o_ref, *, alpha):
    d = x_ref.shape[1]
    y_ssm = _bdot(ys_ref[...], wsu_ref[...])
    y_pool = _bdot(yp_ref[...], wpu_ref[...])
    g = g_ref[...].astype(F32)
    comb = g[:, :d] * y_ssm + g[:, d:] * y_pool
    y = _bdot(comb.astype(BF16), wo_ref[...])
    o_ref[...] = _layer_norm(alpha * x_ref[...] + y, lg_ref[...], lb_ref[...])


def _mix_out(x, ys, yp, gates, w_ssm_up, w_pool_up, w_out, ln_g, ln_b, alpha):
    t, d = x.shape
    tm = TM_PROJ
    row = lambda i: (i, 0)
    full = lambda a: pl.BlockSpec(a.shape, lambda i, n=a.ndim: (0,) * n)
    return pl.pallas_call(
        functools.partial(_mix_out_kernel, alpha=alpha),
        grid=(t // tm,),
        in_specs=[pl.BlockSpec((tm, d), row), pl.BlockSpec((tm, ys.shape[1]), row),
                  pl.BlockSpec((tm, yp.shape[1]), row), pl.BlockSpec((tm, gates.shape[1]), row),
                  full(w_ssm_up), full(w_pool_up), full(w_out), full(ln_g), full(ln_b)],
        out_specs=pl.BlockSpec((tm, d), row),
        out_shape=jax.ShapeDtypeStruct((t, d), F32),
        compiler_params=_params("parallel"),
        name="mix_out",
    )(x, ys, yp, gates, w_ssm_up, w_pool_up, w_out, ln_g, ln_b)


def _kv_kernel(m_ref, wk_ref, wv_ref, k_ref, v_ref):
    mb = m_ref[...].astype(BF16)
    k_ref[...] = _bdot(mb, wk_ref[...]).astype(BF16)
    v_ref[...] = _bdot(mb, wv_ref[...]).astype(BF16)


def _kv_proj(mem2d, wk, wv):
    r, d = mem2d.shape
    tm = 256
    full = lambda a: pl.BlockSpec(a.shape, lambda i: (0, 0))
    return pl.pallas_call(
        _kv_kernel,
        grid=(r // tm,),
        in_specs=[pl.BlockSpec((tm, d), lambda i: (i, 0)), full(wk), full(wv)],
        out_specs=[pl.BlockSpec((tm, d), lambda i: (i, 0))] * 2,
        out_shape=[jax.ShapeDtypeStruct((r, d), BF16)] * 2,
        compiler_params=_params("parallel"),
        name="xattn_kv",
    )(mem2d, wk, wv)


def _xattn_kernel(x_ref, k_ref, v_ref, wq_ref, wo_ref, lg_ref, lb_ref, o_ref, *, alpha):
    x = x_ref[...]
    d = x.shape[1]
    hd = d // N_XHEADS
    q = _bdot(x.astype(BF16), wq_ref[...]) * (hd ** -0.5)
    heads = []
    for h in range(N_XHEADS):
        qh = q[:, h * hd:(h + 1) * hd].astype(BF16)
        kh = k_ref[:, h * hd:(h + 1) * hd]
        s = lax.dot_general(qh, kh, (((1,), (1,)), ((), ())), preferred_element_type=F32)
        s = s - jnp.max(s, axis=-1, keepdims=True)
        p = jnp.exp(s)
        p = p / jnp.sum(p, axis=-1, keepdims=True)
        heads.append(_bdot(p.astype(BF16), v_ref[:, h * hd:(h + 1) * hd]))
    att = jnp.concatenate(heads, axis=1).astype(BF16)
    y = _bdot(att, wo_ref[...])
    o_ref[...] = _layer_norm(alpha * x + y, lg_ref[...], lb_ref[...])


def _xattn(x, k, v, batch, wq, wo, ln_g, ln_b, alpha):
    t, d = x.shape
    n_mem = k.shape[0] // batch
    tm = TM_ATTN
    nt = t // batch // tm
    full = lambda a: pl.BlockSpec(a.shape, lambda b, i, n=a.ndim: (0,) * n)
    row = lambda b, i: (b * nt + i, 0)
    return pl.pallas_call(
        functools.partial(_xattn_kernel, alpha=alpha),
        grid=(batch, nt),
        in_specs=[pl.BlockSpec((tm, d), row),
                  pl.BlockSpec((n_mem, d), lambda b, i: (b, 0)), pl.BlockSpec((n_mem, d), lambda b, i: (b, 0)),
                  full(wq), full(wo), full(ln_g), full(ln_b)],
        out_specs=pl.BlockSpec((tm, d), row),
        out_shape=jax.ShapeDtypeStruct((t, d), F32),
        compiler_params=_params("parallel", "parallel"),
        name="xattn",
    )(x, k, v, wq, wo, ln_g, ln_b)


def _ffn_kernel(x_ref, w1_ref, w3_ref, w2_ref, lg_ref, lb_ref, o_ref, acc_ref, *, alpha):
    f = pl.program_id(1)
    xb = x_ref[...].astype(BF16)
    a = _bdot(xb, w1_ref[...])
    h = (a * _sigmoid(a) * _bdot(xb, w3_ref[...])).astype(BF16)
    part = _bdot(h, w2_ref[...])

    @pl.when(f == 0)
    def _():
        acc_ref[...] = part

    @pl.when(f > 0)
    def _():
        acc_ref[...] += part

    @pl.when(f == pl.num_programs(1) - 1)
    def _():
        o_ref[...] = _layer_norm(alpha * x_ref[...] + acc_ref[...], lg_ref[...], lb_ref[...])


def _ffn_tile(ff):
    for cand in (512, 1408, 1024, 768, 256, 128):
        if ff % cand == 0:
            return cand
    return ff


def _ffn(x, w1, w3, w2, ln_g, ln_b, alpha):
    t, d = x.shape
    ff = w1.shape[1]
    tm, tf = TM_FFN, _ffn_tile(ff)
    full = lambda a: pl.BlockSpec(a.shape, lambda i, f, n=a.ndim: (0,) * n)
    return pl.pallas_call(
        functools.partial(_ffn_kernel, alpha=alpha),
        grid=(t // tm, ff // tf),
        in_specs=[pl.BlockSpec((tm, d), lambda i, f: (i, 0)),
                  pl.BlockSpec((d, tf), lambda i, f: (0, f)), pl.BlockSpec((d, tf), lambda i, f: (0, f)),
                  pl.BlockSpec((tf, d), lambda i, f: (f, 0)), full(ln_g), full(ln_b)],
        out_specs=pl.BlockSpec((tm, d), lambda i, f: (i, 0)),
        out_shape=jax.ShapeDtypeStruct((t, d), F32),
        scratch_shapes=[pltpu.VMEM((tm, d), F32)],
        compiler_params=_params("parallel", "arbitrary"),
        name="ffn_dense",
    )(x, w1, w3, w2, ln_g, ln_b)


def _router_kernel(x_ref, r_ref, o_ref):
    logits = jnp.dot(x_ref[...], r_ref[...], preferred_element_type=F32, precision=lax.Precision.HIGHEST)
    lane = lax.broadcasted_iota(jnp.int32, logits.shape, 1)
    lane_f = lane.astype(F32)
    neg = jnp.float32(-jnp.inf)
    lg = jnp.where(lane < N_EXPERTS, logits, neg)
    m1 = jnp.max(lg, axis=-1, keepdims=True)
    i1 = jnp.min(jnp.where(lg == m1, lane_f, float(LANES)), axis=-1, keepdims=True)
    lg2 = jnp.where(lane_f == i1, neg, lg)
    m2 = jnp.max(lg2, axis=-1, keepdims=True)
    i2 = jnp.min(jnp.where(lg2 == m2, lane_f, float(LANES)), axis=-1, keepdims=True)
    e = jnp.exp(m2 - m1)
    w1 = 1.0 / (1.0 + e)
    w2 = e / (1.0 + e)
    out = jnp.where(lane == 0, w1, jnp.where(lane == 1, w2, jnp.where(lane == 2, i1, jnp.where(lane == 3, i2, 0.0))))
    o_ref[...] = out


def _router(x, router_pad):
    t, d = x.shape
    tm = TM_ROUTER
    return pl.pallas_call(
        _router_kernel,
        grid=(t // tm,),
        in_specs=[pl.BlockSpec((tm, d), lambda i: (i, 0)), pl.BlockSpec(router_pad.shape, lambda i: (0, 0))],
        out_specs=pl.BlockSpec((tm, LANES), lambda i: (i, 0)),
        out_shape=jax.ShapeDtypeStruct((t, LANES), F32),
        compiler_params=_params("parallel"),
        name="moe_router",
    )(x, router_pad)


def _expert_kernel(te_ref, tv_ref, src_hbm, dst_hbm, x_hbm, w1_ref, w3_ref, w2_ref, y_hbm,
                   src_smem, dst_smem, xbuf_ref, xb_ref, acc_ref, obuf_ref, idx_sem, gat_sem, sct_sem):
    i = pl.program_id(0)
    f = pl.program_id(1)
    n_tiles = pl.num_programs(0)
    n_f = pl.num_programs(1)
    tm = xb_ref.shape[0]
    slot = i % 2
    nxt = (i + 1) % 2
    valid = tv_ref[i] == 1

    def idx_copies(tile, s):
        return (pltpu.make_async_copy(src_hbm.at[tile], src_smem.at[s], idx_sem.at[0]),
                pltpu.make_async_copy(dst_hbm.at[tile], dst_smem.at[s], idx_sem.at[1]))

    def issue_gather(s):
        def body(r, c):
            pltpu.make_async_copy(x_hbm.at[pl.ds(src_smem[s, r], 1)], xbuf_ref.at[s, pl.ds(r, 1)],
                                  gat_sem.at[s]).start()
            return c
        lax.fori_loop(0, tm, body, 0, unroll=8)

    def wait_gather(s):
        pltpu.make_async_copy(x_hbm.at[pl.ds(0, tm)], xbuf_ref.at[s], gat_sem.at[s]).wait()

    def wait_scatter(s):
        pltpu.make_async_copy(obuf_ref.at[s], y_hbm.at[pl.ds(0, tm)], sct_sem.at[s]).wait()

    @pl.when(f == 0)
    def _():
        @pl.when(i == 0)
        def _():
            for cp in idx_copies(0, 0):
                cp.start()
            for cp in idx_copies(0, 0):
                cp.wait()
            issue_gather(0)

        wait_gather(slot)
        xb_ref[...] = xbuf_ref[slot].astype(BF16)

        @pl.when(i + 1 < n_tiles)
        def _():
            for cp in idx_copies(i + 1, nxt):
                cp.start()

    @pl.when(jnp.logical_and(f == 1, i + 1 < n_tiles))
    def _():
        for cp in idx_copies(i + 1, nxt):
            cp.wait()
        issue_gather(nxt)

    @pl.when(valid)
    def _():
        xb = xb_ref[...]
        a = _bdot(xb, w1_ref[...])
        h = (a * _sigmoid(a) * _bdot(xb, w3_ref[...])).astype(BF16)
        part = _bdot(h, w2_ref[...])

        @pl.when(f == 0)
        def _():
            acc_ref[...] = part

        @pl.when(f > 0)
        def _():
            acc_ref[...] += part

    @pl.when(f == n_f - 1)
    def _():
        @pl.when(i >= 2)
        def _():
            wait_scatter(slot)

        @pl.when(valid)
        def _():
            obuf_ref[slot] = acc_ref[...]

        @pl.when(jnp.logical_not(valid))
        def _():
            obuf_ref[slot] = jnp.zeros(obuf_ref.shape[1:], obuf_ref.dtype)

        def body(r, c):
            pltpu.make_async_copy(obuf_ref.at[slot, pl.ds(r, 1)], y_hbm.at[pl.ds(dst_smem[slot, r], 1)],
                                  sct_sem.at[slot]).start()
            return c
        lax.fori_loop(0, tm, body, 0, unroll=8)

        @pl.when(i == n_tiles - 1)
        def _():
            wait_scatter(slot)

            @pl.when(n_tiles >= 2)
            def _():
                wait_scatter(nxt)


def _experts(x, src2d, dst2d, tile_expert, tile_valid, w1, w3, w2):
    n_tiles, tm = src2d.shape
    d = x.shape[1]
    ff = w1.shape[2]
    tf = _ffn_tile(ff)
    any_spec = pl.BlockSpec(memory_space=pl.ANY)
    grid_spec = pltpu.PrefetchScalarGridSpec(
        num_scalar_prefetch=2,
        grid=(n_tiles, ff // tf),
        in_specs=[any_spec, any_spec, any_spec,
                  pl.BlockSpec((None, d, tf), lambda i, f, te, tv: (te[i], 0, f)),
                  pl.BlockSpec((None, d, tf), lambda i, f, te, tv: (te[i], 0, f)),
                  pl.BlockSpec((None, tf, d), lambda i, f, te, tv: (te[i], f, 0))],
        out_specs=any_spec,
        scratch_shapes=[pltpu.SMEM((2, tm), jnp.int32), pltpu.SMEM((2, tm), jnp.int32),
                        pltpu.VMEM((2, tm, d), F32),
                        pltpu.VMEM((tm, d), BF16),
                        pltpu.VMEM((tm, d), F32),
                        pltpu.VMEM((2, tm, d), F32),
                        pltpu.SemaphoreType.DMA((2,)), pltpu.SemaphoreType.DMA((2,)),
                        pltpu.SemaphoreType.DMA((2,))],
    )
    return pl.pallas_call(
        _expert_kernel,
        grid_spec=grid_spec,
        out_shape=jax.ShapeDtypeStruct((n_tiles * tm, d), F32),
        compiler_params=_params("arbitrary", "arbitrary"),
        name="moe_experts",
    )(tile_expert, tile_valid, src2d, dst2d, x, w1, w3, w2)


def _combine_kernel(info_ref, x_ref, y0_ref, y1_ref, lg_ref, lb_ref, o_ref, *, alpha):
    info = info_ref[...]
    y = info[:, 0:1] * y0_ref[...] + info[:, 1:2] * y1_ref[...]
    o_ref[...] = _layer_norm(alpha * x_ref[...] + y, lg_ref[...], lb_ref[...])


def _combine(info, x, y, ln_g, ln_b, alpha):
    t, d = x.shape
    tm = TM_COMBINE
    nt = t // tm
    full = lambda a: pl.BlockSpec(a.shape, lambda i, n=a.ndim: (0,) * n)
    return pl.pallas_call(
        functools.partial(_combine_kernel, alpha=alpha),
        grid=(nt,),
        in_specs=[pl.BlockSpec((tm, LANES), lambda i: (i, 0)), pl.BlockSpec((tm, d), lambda i: (i, 0)),
                  pl.BlockSpec((tm, d), lambda i: (i, 0)), pl.BlockSpec((tm, d), lambda i: (i + nt, 0)),
                  full(ln_g), full(ln_b)],
        out_specs=pl.BlockSpec((tm, d), lambda i: (i, 0)),
        out_shape=jax.ShapeDtypeStruct((t, d), F32),
        compiler_params=_params("parallel"),
        name="moe_combine",
    )(info, x, y, y, ln_g, ln_b)


def _moe(x, router, w1, w3, w2, ln_g, ln_b, alpha):
    t, d = x.shape
    ne = router.shape[1]
    router_pad = jnp.pad(router, ((0, 0), (0, LANES - ne)))
    info = _router(x, router_pad)
    top_idx = info[:, 2:4].astype(jnp.int32).reshape(-1)
    onehot = (top_idx[:, None] == jnp.arange(ne, dtype=jnp.int32)[None, :]).astype(jnp.int32)
    rank = jnp.sum((jnp.cumsum(onehot, axis=0) - onehot) * onehot, axis=1)
    counts = jnp.sum(onehot, axis=0)
    padded = ((counts + TM_EXPERT - 1) // TM_EXPERT) * TM_EXPERT
    ends = jnp.cumsum(padded)
    offs = ends - padded
    pos = offs[top_idx] + rank
    n_rows = 2 * t + ne * TM_EXPERT
    assign = jnp.full((n_rows,), -1, jnp.int32).at[pos].set(jnp.arange(2 * t, dtype=jnp.int32))
    real = assign >= 0
    pad_rank = jnp.cumsum(1 - real.astype(jnp.int32)) - 1
    src = jnp.where(real, assign // 2, 0)
    dst = jnp.where(real, (assign % 2) * t + assign // 2, 2 * t + pad_rank)
    tile_start = jnp.arange(n_rows // TM_EXPERT, dtype=jnp.int32) * TM_EXPERT
    tile_valid = (tile_start < ends[-1]).astype(jnp.int32)
    tile_expert = jnp.sum((tile_start[:, None] >= ends[None, :]).astype(jnp.int32), axis=1)
    last_valid = jnp.maximum(ends[-1] // TM_EXPERT - 1, 0)
    tile_expert = jnp.where(tile_valid == 1, jnp.minimum(tile_expert, ne - 1), tile_expert[last_valid])

    y = _experts(x, src.reshape(-1, TM_EXPERT), dst.reshape(-1, TM_EXPERT), tile_expert, tile_valid, w1, w3, w2)
    return _combine(info, x, y, ln_g, ln_b, alpha)


def kernel(x, mem, ln_mix_g, ln_mix_b, w_in, ssm_a_re, ssm_a_im, ssm_log_dt, ssm_b_re, ssm_b_im, ssm_c_re, ssm_c_im, ssm_d, ssm_w_glu, ssm_b_glu, w_ssm_up, pool_w, pool_scale, w_pool_up, w_out, ln_xa_g, ln_xa_b, xa_wq, xa_wk, xa_wv, xa_wo, ln_ffn_g, ln_ffn_b, ffn_w1, ffn_w3, ffn_w2, moe_router, moe_w1, moe_w3, moe_w2):
    batch, seq, d = x.shape
    depth = w_in.shape[0]
    alpha = (2 * depth) ** 0.25
    ssm_w = ssm_d.shape[1]
    pool_width = pool_scale.shape[1]
    t = batch * seq
    assert seq % TS_SCAN == 0 and t % TM_PROJ == 0 and TS_SCAN % SCAN_ROWS == 0

    pos_re, pos_im, neg_re, neg_im, bb_re, bb_im = _ssm_prep(ssm_a_re, ssm_a_im, ssm_log_dt, ssm_b_re, ssm_b_im)
    r = jnp.arange(TS_SCAN)
    tri = ((r[:, None] // SCAN_ROWS == r[None, :] // SCAN_ROWS) & (r[None, :] <= r[:, None])).astype(BF16)
    bf = lambda a: a.astype(BF16)
    row = lambda a: a.reshape(1, -1)

    xf = x.reshape(t, d)
    mem2d = mem.reshape(-1, d)
    for l in range(depth):
        us, up, gates = _mix_in(xf, bf(w_in[l]), ssm_w, pool_width)
        ys, yp = _mix_mid(us, up, batch, _block_diag_in(bb_re[l], bb_im[l]),
                          _block_diag_out(ssm_c_re[l], ssm_c_im[l]),
                          pos_re[l], pos_im[l], neg_re[l], neg_im[l], tri, row(ssm_d[l]), bf(ssm_w_glu[l]),
                          row(ssm_b_glu[l]), bf(pool_w[l]), row(pool_scale[l]))
        xf = _mix_out(xf, ys, yp, gates, bf(w_ssm_up[l]), bf(w_pool_up[l]), bf(w_out[l]),
                      row(ln_mix_g[l]), row(ln_mix_b[l]), alpha)
        k, v = _kv_proj(mem2d, bf(xa_wk[l]), bf(xa_wv[l]))
        xf = _xattn(xf, k, v, batch, bf(xa_wq[l]), bf(xa_wo[l]), row(ln_xa_g[l]), row(ln_xa_b[l]), alpha)
        if l % 2 == 0:
            i = l // 2
            xf = _ffn(xf, bf(ffn_w1[i]), bf(ffn_w3[i]), bf(ffn_w2[i]), row(ln_ffn_g[l]), row(ln_ffn_b[l]), alpha)
        else:
            i = l // 2
            xf = _moe(xf, moe_router[i], bf(moe_w1[i]), bf(moe_w3[i]), bf(moe_w2[i]),
                      row(ln_ffn_g[l]), row(ln_ffn_b[l]), alpha)
    return xf.reshape(batch, seq, d)
```

```python
import functools
import math

import jax
import jax.numpy as jnp
from jax import lax
from jax.experimental import pallas as pl
from jax.experimental.pallas import tpu as pltpu

F32 = jnp.float32
BF16 = jnp.bfloat16

LN_EPS = 1e-5
SSM_GROUP = 16
SSM_STATE = 64
GROUPS_PER_BLOCK = 8
SCAN_ROWS = 16
BLK_TABLE_ROWS = 8
POOL_WINDOWS = (2, 4, 8, 16)
POOL_HALO = 16
N_XHEADS = 4
N_EXPERTS = 8
LANES = 128

VMEM_LIMIT_BYTES = 56 * 1024 * 1024

TM_PROJ = 512
TS_SCAN = 256
SCAN_LEVELS = (TS_SCAN // SCAN_ROWS).bit_length() - 1
assert TS_SCAN == SCAN_ROWS << SCAN_LEVELS and SCAN_LEVELS <= BLK_TABLE_ROWS
TM_ATTN = 512
TM_FFN = 512
TM_ROUTER = 512
TM_EXPERT = 896
TM_COMBINE = 256


def _params(*sem):
    return pltpu.CompilerParams(dimension_semantics=sem, vmem_limit_bytes=VMEM_LIMIT_BYTES)


def _layer_norm(z, g, b):
    mu = jnp.mean(z, axis=-1, keepdims=True)
    zc = z - mu
    var = jnp.mean(zc * zc, axis=-1, keepdims=True)
    return zc * lax.rsqrt(var + LN_EPS) * g + b


def _sigmoid(v):
    return 1.0 / (1.0 + jnp.exp(-v))


def _bdot(a, b):
    return jnp.dot(a, b, preferred_element_type=F32)


def _ssm_prep_kernel(a_re_ref, a_im_ref, log_dt_ref, b_re_ref, b_im_ref,
                     pos_re_ref, pos_im_ref, neg_re_ref, neg_im_ref, blk_re_ref, blk_im_ref, bb_re_ref, bb_im_ref):
    a_re = a_re_ref[...]
    a_im = a_im_ref[...]
    dt = jnp.exp(log_dt_ref[...])
    mag = jnp.exp(a_re * dt)
    lam_re = mag * jnp.cos(a_im * dt)
    lam_im = mag * jnp.sin(a_im * dt)
    den = a_re * a_re + a_im * a_im
    num_re = lam_re - 1.0
    f_re = (num_re * a_re + lam_im * a_im) / den
    f_im = (lam_im * a_re - num_re * a_im) / den
    b_re = b_re_ref[...]
    b_im = b_im_ref[...]
    bb_re_ref[...] = f_re[:, None, :] * b_re - f_im[:, None, :] * b_im
    bb_im_ref[...] = f_re[:, None, :] * b_im + f_im[:, None, :] * b_re
    m2 = lam_re * lam_re + lam_im * lam_im
    inv_re = lam_re / m2
    inv_im = -lam_im / m2
    p_re, p_im = lam_re, lam_im
    n_re, n_im = inv_re, inv_im
    for k in range(SCAN_ROWS):
        pos_re_ref[k] = p_re
        pos_im_ref[k] = p_im
        neg_re_ref[k] = n_re
        neg_im_ref[k] = n_im
        if k + 1 < SCAN_ROWS:
            p_re, p_im = p_re * lam_re - p_im * lam_im, p_re * lam_im + p_im * lam_re
            n_re, n_im = n_re * inv_re - n_im * inv_im, n_re * inv_im + n_im * inv_re
    q_re, q_im = p_re, p_im
    for k in range(blk_re_ref.shape[0]):
        if k < SCAN_LEVELS:
            blk_re_ref[k] = q_re
            blk_im_ref[k] = q_im
            q_re, q_im = q_re * q_re - q_im * q_im, 2.0 * q_re * q_im
        else:
            blk_re_ref[k] = jnp.zeros_like(q_re)
            blk_im_ref[k] = jnp.zeros_like(q_im)


def _ssm_prep(a_re, a_im, log_dt, b_re, b_im):
    nl, g, p = a_re.shape
    c = b_re.shape[-1]
    lg = nl * g
    tab = jax.ShapeDtypeStruct((SCAN_ROWS, lg, p), F32)
    blk = jax.ShapeDtypeStruct((BLK_TABLE_ROWS, lg, p), F32)
    bbs = jax.ShapeDtypeStruct((lg, c, p), F32)
    outs = pl.pallas_call(
        _ssm_prep_kernel,
        out_shape=(tab,) * 4 + (blk, blk, bbs, bbs),
        name="ssm_prep",
    )(a_re.reshape(lg, p), a_im.reshape(lg, p), log_dt.reshape(lg, 1),
      b_re.transpose(0, 1, 3, 2).reshape(lg, c, p), b_im.transpose(0, 1, 3, 2).reshape(lg, c, p))

    def tab_layout(t):
        return t.reshape(t.shape[0], nl, g * p).transpose(1, 0, 2)

    return tuple(tab_layout(t) for t in outs[:6]) + (outs[6].reshape(nl, g, c, p), outs[7].reshape(nl, g, c, p))


def _block_diag_in(bb_re, bb_im):
    g, c, p = bb_re.shape
    nb = g // GROUPS_PER_BLOCK
    eye = jnp.eye(GROUPS_PER_BLOCK, dtype=F32)

    def one(bb):
        bb = bb.reshape(nb, GROUPS_PER_BLOCK, c, p)
        m = bb[:, :, :, None, :] * eye[None, :, None, :, None]
        return m.reshape(nb, GROUPS_PER_BLOCK * c, GROUPS_PER_BLOCK * p)

    return jnp.concatenate([one(bb_re), one(bb_im)], axis=-1).astype(BF16)


def _block_diag_out(c_re, c_im):
    g, c, p = c_re.shape
    nb = g // GROUPS_PER_BLOCK
    eye = jnp.eye(GROUPS_PER_BLOCK, dtype=F32)

    def one(cm):
        cm = cm.reshape(nb, GROUPS_PER_BLOCK, c, p).transpose(0, 1, 3, 2)
        m = cm[:, :, :, None, :] * eye[None, :, None, :, None]
        return m.reshape(nb, GROUPS_PER_BLOCK * p, GROUPS_PER_BLOCK * c)

    return jnp.concatenate([one(c_re), -one(c_im)], axis=1).astype(BF16)


def _mix_in_kernel(x_ref, w_ref, us_ref, up_ref, g_ref, *, ssm_w, pool_w):
    xb = x_ref[...].astype(BF16)
    us_ref[...] = _bdot(xb, w_ref[:, :ssm_w])
    up_ref[...] = _bdot(xb, w_ref[:, ssm_w:ssm_w + pool_w])
    g_ref[...] = _sigmoid(_bdot(xb, w_ref[:, ssm_w + pool_w:])).astype(BF16)


def _mix_in(x, w_in, ssm_w, pool_w):
    t, d = x.shape
    cols = w_in.shape[1]
    gate_w = cols - ssm_w - pool_w
    tm = TM_PROJ
    return pl.pallas_call(
        functools.partial(_mix_in_kernel, ssm_w=ssm_w, pool_w=pool_w),
        grid=(t // tm,),
        in_specs=[pl.BlockSpec((tm, d), lambda i: (i, 0)),
                  pl.BlockSpec((d, cols), lambda i: (0, 0))],
        out_specs=[pl.BlockSpec((tm, ssm_w), lambda i: (i, 0)),
                   pl.BlockSpec((tm, pool_w), lambda i: (i, 0)),
                   pl.BlockSpec((tm, gate_w), lambda i: (i, 0))],
        out_shape=[jax.ShapeDtypeStruct((t, ssm_w), F32),
                   jax.ShapeDtypeStruct((t, pool_w), F32),
                   jax.ShapeDtypeStruct((t, gate_w), BF16)],
        compiler_params=_params("parallel"),
        name="mix_in",
    )(x, w_in)


def _mix_mid_kernel(us_ref, up_ref, bin_ref, cout_ref, pos_re_ref, pos_im_ref, neg_re_ref, neg_im_ref,
                    blk_re_ref, blk_im_ref, tri_ref, d_ref, wglu_ref, bglu_ref, pw_ref, ps_ref,
                    ys_ref, yp_ref, w_ref, halo_ref):
    ti = pl.program_id(1)
    ts = us_ref.shape[0]
    n_blocks = bin_ref.shape[0]
    half = bin_ref.shape[2] // 2
    ch = bin_ref.shape[1]
    nb = ts // SCAN_ROWS

    @pl.when(ti == 0)
    def _():
        w_ref[:, 0:SCAN_ROWS, :] = jnp.zeros((w_ref.shape[0], SCAN_ROWS, LANES), F32)
        halo_ref[...] = jnp.zeros_like(halo_ref)

    u = us_ref[...]
    ub = u.astype(BF16)
    blk_row = lax.broadcasted_iota(jnp.int32, (nb, half), 0)
    ys = []
    for j in range(n_blocks):
        c0 = j * 2 * half
        sl = slice(j * half, (j + 1) * half)
        bu = _bdot(ub[:, j * ch:(j + 1) * ch], bin_ref[j]).astype(BF16)
        bu_re = bu[:, :half].reshape(nb, SCAN_ROWS, half)
        bu_im = bu[:, half:].reshape(nb, SCAN_ROWS, half)
        n_re = neg_re_ref[:, sl][None]
        n_im = neg_im_ref[:, sl][None]
        z_re = (n_re * bu_re - n_im * bu_im).reshape(ts, half)
        z_im = (n_re * bu_im + n_im * bu_re).reshape(ts, half)
        z = jnp.concatenate([z_re, z_im], axis=1)
        w = _bdot(tri_ref[...], z)
        n_ch = 2 * half // LANES
        for c in range(n_ch):
            w_ref[j * n_ch + c, SCAN_ROWS:, :] = w[:, c * LANES:(c + 1) * LANES]
        e = [w_ref[j * n_ch + c, pl.ds(SCAN_ROWS - 1, nb, stride=SCAN_ROWS), :] for c in range(n_ch)]
        e_re = jnp.concatenate(e[:n_ch // 2], axis=1)
        e_im = jnp.concatenate(e[n_ch // 2:], axis=1)
        a_re = blk_re_ref[0:1, sl]
        a_im = blk_im_ref[0:1, sl]
        c_re = a_re * e_re - a_im * e_im
        c_im = a_re * e_im + a_im * e_re
        for k in range(SCAN_LEVELS):
            sh = 1 << k
            q_re = blk_re_ref[k:k + 1, sl]
            q_im = blk_im_ref[k:k + 1, sl]
            r_re = jnp.where(blk_row >= sh, pltpu.roll(c_re, sh, 0), 0.0)
            r_im = jnp.where(blk_row >= sh, pltpu.roll(c_im, sh, 0), 0.0)
            c_re, c_im = c_re + (q_re * r_re - q_im * r_im), c_im + (q_re * r_im + q_im * r_re)
        w_re = w[:, :half].reshape(nb, SCAN_ROWS, half)
        w_im = w[:, half:].reshape(nb, SCAN_ROWS, half)
        s_re = w_re + c_re[:, None, :]
        s_im = w_im + c_im[:, None, :]
        last = jnp.concatenate([s_re[nb - 1, SCAN_ROWS - 1:, :], s_im[nb - 1, SCAN_ROWS - 1:, :]], axis=1)
        for c in range(n_ch):
            w_ref[j * n_ch + c, SCAN_ROWS - 1:SCAN_ROWS, :] = last[:, c * LANES:(c + 1) * LANES]
        p_re = pos_re_ref[:, sl][None]
        p_im = pos_im_ref[:, sl][None]
        s_re = s_re.astype(BF16)
        s_im = s_im.astype(BF16)
        h_re = (p_re * s_re - p_im * s_im).reshape(ts, half)
        h_im = (p_re * s_im + p_im * s_re).reshape(ts, half)
        h = jnp.concatenate([h_re, h_im], axis=1)
        ys.append(_bdot(h, cout_ref[j]))
    y = jnp.concatenate(ys, axis=1) + d_ref[...] * u
    y = jax.nn.gelu(y)
    y = y * _sigmoid(_bdot(y.astype(BF16), wglu_ref[...]) + bglu_ref[...])
    ys_ref[...] = y.astype(ys_ref.dtype)

    up = up_ref[...]
    ext = jnp.concatenate([halo_ref[...], up], axis=0)
    halo_ref[...] = up[ts - POOL_HALO:, :]
    pg = pw_ref.shape[1]
    t_pos = (ti * ts + lax.broadcasted_iota(jnp.int32, (ts, 1), 0) + 1).astype(F32)
    outs = []
    for gi, win in enumerate(POOL_WINDOWS):
        e = ext[:, gi * pg:(gi + 1) * pg]
        s = e
        sh = 1
        while sh < win:
            s = s + pltpu.roll(s, sh, 0)
            sh *= 2
        mean = s[POOL_HALO:, :] / jnp.minimum(t_pos, float(win))
        dlt = (mean - e[POOL_HALO:, :]).astype(BF16)
        outs.append(_bdot(dlt, pw_ref[gi]))
    yp = jnp.concatenate(outs, axis=1) * ps_ref[...]
    yp_ref[...] = yp.astype(yp_ref.dtype)


def _mix_mid(us, up, batch, bin_blk, cout_blk, tables, tri, ssm_d, w_glu, b_glu, pool_w, pool_scale):
    t, sw = us.shape
    pw = up.shape[1]
    seq = t // batch
    ts = TS_SCAN
    nt = seq // ts
    n_state = tables[0].shape[1]

    def full(a):
        return pl.BlockSpec(a.shape, lambda b, i, n=a.ndim: (0,) * n)

    row = lambda b, i: (b * nt + i, 0)
    consts = (bin_blk, cout_blk) + tuple(tables) + (tri, ssm_d, w_glu, b_glu, pool_w, pool_scale)
    return pl.pallas_call(
        _mix_mid_kernel,
        grid=(batch, nt),
        in_specs=[pl.BlockSpec((ts, sw), row), pl.BlockSpec((ts, pw), row)] + [full(a) for a in consts],
        out_specs=[pl.BlockSpec((ts, sw), row), pl.BlockSpec((ts, pw), row)],
        out_shape=[jax.ShapeDtypeStruct((t, sw), BF16), jax.ShapeDtypeStruct((t, pw), BF16)],
        scratch_shapes=[pltpu.VMEM((2 * n_state // LANES, SCAN_ROWS + ts, LANES), F32),
                        pltpu.VMEM((POOL_HALO, pw), F32)],
        compiler_params=_params("parallel", "arbitrary"),
        name="mix_mid",
    )(us, up, *consts)


def _mix_out_kernel(x_ref, ys_ref, yp_ref, g_ref, wsu_ref, wpu_ref, wo_ref, lg_ref, lb_ref, o_ref, *, alpha):
    d = x_ref.shape[1]
    y_ssm = _bdot(ys_ref[...], wsu_ref[...])
    y_pool = _bdot(yp_ref[...], wpu_ref[...])
    g = g_ref[...].astype(F32)
    comb = g[:, :d] * y_ssm + g[:, d:] * y_pool
    y = _bdot(comb.astype(BF16), wo_ref[...])
    o_ref[...] = _layer_norm(alpha * x_ref[...] + y, lg_ref[...], lb_ref[...])


def _mix_out(x, ys, yp, gates, w_ssm_up, w_pool_up, w_out, ln_g, ln_b, alpha):
    t, d = x.shape
    tm = TM_PROJ
    row = lambda i: (i, 0)
    full = lambda a: pl.BlockSpec(a.shape, lambda i, n=a.ndim: (0,) * n)
    return pl.pallas_call(
        functools.partial(_mix_out_kernel, alpha=alpha),
        grid=(t // tm,),
        in_specs=[pl.BlockSpec((tm, d), row), pl.BlockSpec((tm, ys.shape[1]), row),
                  pl.BlockSpec((tm, yp.shape[1]), row), pl.BlockSpec((tm, gates.shape[1]), row),
                  full(w_ssm_up), full(w_pool_up), full(w_out), full(ln_g), full(ln_b)],
        out_specs=pl.BlockSpec((tm, d), row),
        out_shape=jax.ShapeDtypeStruct((t, d), F32),
        compiler_params=_params("parallel"),
        name="mix_out",
    )(x, ys, yp, gates, w_ssm_up, w_pool_up, w_out, ln_g, ln_b)


def _kv_kernel(m_ref, wk_ref, wv_ref, k_ref, v_ref):
    mb = m_ref[...].astype(BF16)
    k_ref[...] = _bdot(mb, wk_ref[...]).astype(BF16)
    v_ref[...] = _bdot(mb, wv_ref[...]).astype(BF16)


def _kv_proj(mem2d, wk, wv):
    r, d = mem2d.shape
    tm = 256
    full = lambda a: pl.BlockSpec(a.shape, lambda i: (0, 0))
    return pl.pallas_call(
        _kv_kernel,
        grid=(r // tm,),
        in_specs=[pl.BlockSpec((tm, d), lambda i: (i, 0)), full(wk), full(wv)],
        out_specs=[pl.BlockSpec((tm, d), lambda i: (i, 0))] * 2,
        out_shape=[jax.ShapeDtypeStruct((r, d), BF16)] * 2,
        compiler_params=_params("parallel"),
        name="xattn_kv",
    )(mem2d, wk, wv)


def _xattn_kernel(x_ref, k_ref, v_ref, wq_ref, wo_ref, lg_ref, lb_ref, o_ref, *, alpha):
    x = x_ref[...]
    d = x.shape[1]
    hd = d // N_XHEADS
    q = _bdot(x.astype(BF16), wq_ref[...]) * (hd ** -0.5)
    heads = []
    for h in range(N_XHEADS):
        qh = q[:, h * hd:(h + 1) * hd].astype(BF16)
        kh = k_ref[:, h * hd:(h + 1) * hd]
        s = lax.dot_general(qh, kh, (((1,), (1,)), ((), ())), preferred_element_type=F32)
        s = s - jnp.max(s, axis=-1, keepdims=True)
        p = jnp.exp(s)
        p = p / jnp.sum(p, axis=-1, keepdims=True)
        heads.append(_bdot(p.astype(BF16), v_ref[:, h * hd:(h + 1) * hd]))
    att = jnp.concatenate(heads, axis=1).astype(BF16)
    y = _bdot(att, wo_ref[...])
    o_ref[...] = _layer_norm(alpha * x + y, lg_ref[...], lb_ref[...])


def _xattn(x, k, v, batch, wq, wo, ln_g, ln_b, alpha):
    t, d = x.shape
    n_mem = k.shape[0] // batch
    tm = TM_ATTN
    nt = t // batch // tm
    full = lambda a: pl.BlockSpec(a.shape, lambda b, i, n=a.ndim: (0,) * n)
    row = lambda b, i: (b * nt + i, 0)
    return pl.pallas_call(
        functools.partial(_xattn_kernel, alpha=alpha),
        grid=(batch, nt),
        in_specs=[pl.BlockSpec((tm, d), row),
                  pl.BlockSpec((n_mem, d), lambda b, i: (b, 0)), pl.BlockSpec((n_mem, d), lambda b, i: (b, 0)),
                  full(wq), full(wo), full(ln_g), full(ln_b)],
        out_specs=pl.BlockSpec((tm, d), row),
        out_shape=jax.ShapeDtypeStruct((t, d), F32),
        compiler_params=_params("parallel", "parallel"),
        name="xattn",
    )(x, k, v, wq, wo, ln_g, ln_b)


def _ffn_kernel(x_ref, w1_ref, w3_ref, w2_ref, lg_ref, lb_ref, o_ref, acc_ref, *, alpha):
    f = pl.program_id(1)
    xb = x_ref[...].astype(BF16)
    a = _bdot(xb, w1_ref[...])
    h = (a * _sigmoid(a) * _bdot(xb, w3_ref[...])).astype(BF16)
    part = _bdot(h, w2_ref[...])

    @pl.when(f == 0)
    def _():
        acc_ref[...] = part

    @pl.when(f > 0)
    def _():
        acc_ref[...] += part

    @pl.when(f == pl.num_programs(1) - 1)
    def _():
        o_ref[...] = _layer_norm(alpha * x_ref[...] + acc_ref[...], lg_ref[...], lb_ref[...])


def _ffn_tile(ff):
    for cand in (512, 1408, 1024, 768, 256, 128):
        if ff % cand == 0:
            return cand
    return ff


def _ffn(x, w1, w3, w2, ln_g, ln_b, alpha):
    t, d = x.shape
    ff = w1.shape[1]
    tm, tf = TM_FFN, _ffn_tile(ff)
    full = lambda a: pl.BlockSpec(a.shape, lambda i, f, n=a.ndim: (0,) * n)
    return pl.pallas_call(
        functools.partial(_ffn_kernel, alpha=alpha),
        grid=(t // tm, ff // tf),
        in_specs=[pl.BlockSpec((tm, d), lambda i, f: (i, 0)),
                  pl.BlockSpec((d, tf), lambda i, f: (0, f)), pl.BlockSpec((d, tf), lambda i, f: (0, f)),
                  pl.BlockSpec((tf, d), lambda i, f: (f, 0)), full(ln_g), full(ln_b)],
        out_specs=pl.BlockSpec((tm, d), lambda i, f: (i, 0)),
        out_shape=jax.ShapeDtypeStruct((t, d), F32),
        scratch_shapes=[pltpu.VMEM((tm, d), F32)],
        compiler_params=_params("parallel", "arbitrary"),
        name="ffn_dense",
    )(x, w1, w3, w2, ln_g, ln_b)


def _router_kernel(x_ref, r_ref, o_ref):
    logits = jnp.dot(x_ref[...], r_ref[...], preferred_element_type=F32, precision=lax.Precision.HIGHEST)
    lane = lax.broadcasted_iota(jnp.int32, logits.shape, 1)
    lane_f = lane.astype(F32)
    neg = jnp.float32(-jnp.inf)
    lg = jnp.where(lane < N_EXPERTS, logits, neg)
    m1 = jnp.max(lg, axis=-1, keepdims=True)
    i1 = jnp.min(jnp.where(lg == m1, lane_f, float(LANES)), axis=-1, keepdims=True)
    lg2 = jnp.where(lane_f == i1, neg, lg)
    m2 = jnp.max(lg2, axis=-1, keepdims=True)
    i2 = jnp.min(jnp.where(lg2 == m2, lane_f, float(LANES)), axis=-1, keepdims=True)
    e = jnp.exp(m2 - m1)
    w1 = 1.0 / (1.0 + e)
    w2 = e / (1.0 + e)
    out = jnp.where(lane == 0, w1, jnp.where(lane == 1, w2, jnp.where(lane == 2, i1, jnp.where(lane == 3, i2, 0.0))))
    o_ref[...] = out


def _router(x, router_pad):
    t, d = x.shape
    tm = TM_ROUTER
    return pl.pallas_call(
        _router_kernel,
        grid=(t // tm,),
        in_specs=[pl.BlockSpec((tm, d), lambda i: (i, 0)), pl.BlockSpec(router_pad.shape, lambda i: (0, 0))],
        out_specs=pl.BlockSpec((tm, LANES), lambda i: (i, 0)),
        out_shape=jax.ShapeDtypeStruct((t, LANES), F32),
        compiler_params=_params("parallel"),
        name="moe_router",
    )(x, router_pad)


def _expert_kernel(te_ref, src_hbm, dst_hbm, x_hbm, w1_ref, w3_ref, w2_ref, y_hbm,
                   src_smem, dst_smem, xbuf_ref, xb_ref, acc_ref, obuf_ref, src_sem, dst_sem, gat_sem, sct_sem,
                   *, n_f):
    i = pl.program_id(0)
    f = pl.program_id(1)
    n_tiles = pl.num_programs(0)
    tm = xb_ref.shape[0]
    chunk = tm // n_f
    slot = i % 2
    nxt = (i + 1) % 2

    def src_copy(tile, s):
        return pltpu.make_async_copy(src_hbm.at[tile], src_smem.at[s], src_sem)

    def dst_copy(tile, s):
        return pltpu.make_async_copy(dst_hbm.at[tile + 1], dst_smem.at[s], dst_sem)

    def gather_row(s, r):
        return pltpu.make_async_copy(x_hbm.at[pl.ds(src_smem[s, r], 1)], xbuf_ref.at[s, pl.ds(r, 1)], gat_sem.at[s])

    def scatter_row(s, r):
        return pltpu.make_async_copy(obuf_ref.at[s, pl.ds(r, 1)], y_hbm.at[pl.ds(dst_smem[s, r], 1)], sct_sem.at[s])

    def wait_gather(s):
        pltpu.make_async_copy(x_hbm.at[pl.ds(0, tm)], xbuf_ref.at[s], gat_sem.at[s]).wait()

    def wait_scatter(s):
        pltpu.make_async_copy(obuf_ref.at[s], y_hbm.at[pl.ds(0, tm)], sct_sem.at[s]).wait()

    @pl.when(f == 0)
    def _():
        @pl.when(i == 0)
        def _():
            for cp in (src_copy(0, 0), src_copy(1, 1), dst_copy(-1, 1)):
                cp.start()
                cp.wait()
            obuf_ref[1] = jnp.zeros(obuf_ref.shape[1:], obuf_ref.dtype)

            def body(r, c):
                gather_row(0, r).start()
                return c
            lax.fori_loop(0, tm, body, 0, unroll=8)

        @pl.when(i > 0)
        def _():
            src_copy(i + 1, nxt).wait()
            dst_copy(i - 1, nxt).wait()

        src_copy(i + 2, slot).start()
        dst_copy(i, slot).start()
        wait_gather(slot)
        xb_ref[...] = xbuf_ref[slot].astype(BF16)

    r0 = f * chunk
    for k in range(chunk):
        gather_row(nxt, r0 + k).start()
        scatter_row(nxt, r0 + k).start()

    xb = xb_ref[...]
    a = _bdot(xb, w1_ref[...])
    h = (a * _sigmoid(a) * _bdot(xb, w3_ref[...])).astype(BF16)
    part = _bdot(h, w2_ref[...])

    @pl.when(f == 0)
    def _():
        acc_ref[...] = part

    @pl.when(jnp.logical_and(f > 0, f < n_f - 1))
    def _():
        acc_ref[...] += part

    @pl.when(f == n_f - 1)
    def _():
        @pl.when(i > 0)
        def _():
            wait_scatter(slot)

        obuf_ref[slot] = acc_ref[...] + part

        @pl.when(i == n_tiles - 1)
        def _():
            wait_scatter(nxt)
            dst_copy(i, slot).wait()

            def body(r, c):
                scatter_row(slot, r).start()
                return c
            lax.fori_loop(0, tm, body, 0, unroll=8)
            wait_scatter(slot)
            wait_gather(nxt)
            src_copy(i + 2, slot).wait()


def _experts(x, src2d, dst2d, tile_expert, w1, w3, w2):
    n_tiles = tile_expert.shape[0]
    tm = src2d.shape[1]
    d = x.shape[1]
    ff = w1.shape[2]
    tf = _ffn_tile(ff)
    n_f = ff // tf
    assert n_f >= 2 and tm % n_f == 0 and src2d.shape[0] == n_tiles + 2 and dst2d.shape[0] == n_tiles + 1
    any_spec = pl.BlockSpec(memory_space=pl.ANY)
    grid_spec = pltpu.PrefetchScalarGridSpec(
        num_scalar_prefetch=1,
        grid=(n_tiles, n_f),
        in_specs=[any_spec, any_spec, any_spec,
                  pl.BlockSpec((None, d, tf), lambda i, f, te: (te[i], 0, f)),
                  pl.BlockSpec((None, d, tf), lambda i, f, te: (te[i], 0, f)),
                  pl.BlockSpec((None, tf, d), lambda i, f, te: (te[i], f, 0))],
        out_specs=any_spec,
        scratch_shapes=[pltpu.SMEM((2, tm), jnp.int32), pltpu.SMEM((2, tm), jnp.int32),
                        pltpu.VMEM((2, tm, d), F32),
                        pltpu.VMEM((tm, d), BF16),
                        pltpu.VMEM((tm, d), F32),
                        pltpu.VMEM((2, tm, d), F32),
                        pltpu.SemaphoreType.DMA, pltpu.SemaphoreType.DMA,
                        pltpu.SemaphoreType.DMA((2,)), pltpu.SemaphoreType.DMA((2,))],
    )
    return pl.pallas_call(
        functools.partial(_expert_kernel, n_f=n_f),
        grid_spec=grid_spec,
        out_shape=jax.ShapeDtypeStruct(((n_tiles + 1) * tm, d), F32),
        compiler_params=_params("arbitrary", "arbitrary"),
        name="moe_experts",
    )(tile_expert, src2d, dst2d, x, w1, w3, w2)


def _combine_kernel(info_ref, x_ref, y0_ref, y1_ref, lg_ref, lb_ref, o_ref, *, alpha):
    info = info_ref[...]
    y = info[:, 0:1] * y0_ref[...] + info[:, 1:2] * y1_ref[...]
    o_ref[...] = _layer_norm(alpha * x_ref[...] + y, lg_ref[...], lb_ref[...])


def _combine(info, x, y, ln_g, ln_b, alpha):
    t, d = x.shape
    tm = TM_COMBINE
    nt = t // tm
    full = lambda a: pl.BlockSpec(a.shape, lambda i, n=a.ndim: (0,) * n)
    return pl.pallas_call(
        functools.partial(_combine_kernel, alpha=alpha),
        grid=(nt,),
        in_specs=[pl.BlockSpec((tm, LANES), lambda i: (i, 0)), pl.BlockSpec((tm, d), lambda i: (i, 0)),
                  pl.BlockSpec((tm, d), lambda i: (i, 0)), pl.BlockSpec((tm, d), lambda i: (i + nt, 0)),
                  full(ln_g), full(ln_b)],
        out_specs=pl.BlockSpec((tm, d), lambda i: (i, 0)),
        out_shape=jax.ShapeDtypeStruct((t, d), F32),
        compiler_params=_params("parallel"),
        name="moe_combine",
    )(info, x, y, y, ln_g, ln_b)


def _moe(x, router, w1, w3, w2, ln_g, ln_b, alpha):
    t, d = x.shape
    ne = router.shape[1]
    router_pad = jnp.pad(router, ((0, 0), (0, LANES - ne)))
    info = _router(x, router_pad)
    top_idx = info[:, 2:4].astype(jnp.int32).reshape(-1)
    onehot = (top_idx[:, None] == jnp.arange(ne, dtype=jnp.int32)[None, :]).astype(jnp.int32)
    rank = jnp.sum((jnp.cumsum(onehot, axis=0) - onehot) * onehot, axis=1)
    counts = jnp.sum(onehot, axis=0)
    padded = ((counts + TM_EXPERT - 1) // TM_EXPERT) * TM_EXPERT
    ends = jnp.cumsum(padded)
    offs = ends - padded
    pos = offs[top_idx] + rank
    n_tiles = (2 * t + ne * (TM_EXPERT - 1) + TM_EXPERT - 1) // TM_EXPERT
    n_rows = n_tiles * TM_EXPERT
    assign = jnp.full((n_rows,), -1, jnp.int32).at[pos].set(jnp.arange(2 * t, dtype=jnp.int32))
    real = assign >= 0
    pad_rank = jnp.cumsum(1 - real.astype(jnp.int32)) - 1
    src = jnp.where(real, assign // 2, 0)
    dst = jnp.where(real, (assign % 2) * t + assign // 2, 2 * t + pad_rank)
    tile_start = jnp.arange(n_tiles, dtype=jnp.int32) * TM_EXPERT
    tile_used = tile_start < ends[-1]
    tile_expert = jnp.sum((tile_start[:, None] >= ends[None, :]).astype(jnp.int32), axis=1)
    last_used = jnp.maximum(ends[-1] // TM_EXPERT - 1, 0)
    tile_expert = jnp.where(tile_used, jnp.minimum(tile_expert, ne - 1), tile_expert[last_used])
    src2d = jnp.concatenate([src.reshape(n_tiles, TM_EXPERT), jnp.zeros((2, TM_EXPERT), jnp.int32)], axis=0)
    spare = n_rows + jnp.arange(TM_EXPERT, dtype=jnp.int32)
    dst2d = jnp.concatenate([spare[None, :], dst.reshape(n_tiles, TM_EXPERT)], axis=0)

    y = _experts(x, src2d, dst2d, tile_expert, w1, w3, w2)
    return _combine(info, x, y, ln_g, ln_b, alpha)


def kernel(x, mem, ln_mix_g, ln_mix_b, w_in, ssm_a_re, ssm_a_im, ssm_log_dt, ssm_b_re, ssm_b_im, ssm_c_re, ssm_c_im, ssm_d, ssm_w_glu, ssm_b_glu, w_ssm_up, pool_w, pool_scale, w_pool_up, w_out, ln_xa_g, ln_xa_b, xa_wq, xa_wk, xa_wv, xa_wo, ln_ffn_g, ln_ffn_b, ffn_w1, ffn_w3, ffn_w2, moe_router, moe_w1, moe_w3, moe_w2):
    batch, seq, d = x.shape
    depth = w_in.shape[0]
    alpha = (2 * depth) ** 0.25
    ssm_w = ssm_d.shape[1]
    pool_width = pool_scale.shape[1]
    t = batch * seq
    assert seq % TS_SCAN == 0 and t % TM_PROJ == 0 and TS_SCAN % SCAN_ROWS == 0

    *ssm_tables, bb_re, bb_im = _ssm_prep(ssm_a_re, ssm_a_im, ssm_log_dt, ssm_b_re, ssm_b_im)
    r = jnp.arange(TS_SCAN)
    tri = ((r[:, None] // SCAN_ROWS == r[None, :] // SCAN_ROWS) & (r[None, :] <= r[:, None])).astype(BF16)
    bf = lambda a: a.astype(BF16)
    row = lambda a: a.reshape(1, -1)

    xf = x.reshape(t, d)
    mem2d = mem.reshape(-1, d)
    for l in range(depth):
        us, up, gates = _mix_in(xf, bf(w_in[l]), ssm_w, pool_width)
        ys, yp = _mix_mid(us, up, batch, _block_diag_in(bb_re[l], bb_im[l]),
                          _block_diag_out(ssm_c_re[l], ssm_c_im[l]),
                          [bf(tab[l]) for tab in ssm_tables[:4]] + [tab[l] for tab in ssm_tables[4:]],
                          tri, row(ssm_d[l]), bf(ssm_w_glu[l]),
                          row(ssm_b_glu[l]), bf(pool_w[l]), row(pool_scale[l]))
        xf = _mix_out(xf, ys, yp, gates, bf(w_ssm_up[l]), bf(w_pool_up[l]), bf(w_out[l]),
                      row(ln_mix_g[l]), row(ln_mix_b[l]), alpha)
        k, v = _kv_proj(mem2d, bf(xa_wk[l]), bf(xa_wv[l]))
        xf = _xattn(xf, k, v, batch, bf(xa_wq[l]), bf(xa_wo[l]), row(ln_xa_g[l]), row(ln_xa_b[l]), alpha)
        if l % 2 == 0:
            i = l // 2
            xf = _ffn(xf, bf(ffn_w1[i]), bf(ffn_w3[i]), bf(ffn_w2[i]), row(ln_ffn_g[l]), row(ln_ffn_b[l]), alpha)
        else:
            i = l // 2
            xf = _moe(xf, moe_router[i], bf(moe_w1[i]), bf(moe_w3[i]), bf(moe_w2[i]),
                      row(ln_ffn_g[l]), row(ln_ffn_b[l]), alpha)
    return xf.reshape(batch, seq, d)
```

```python
import functools
import math

import jax
import jax.numpy as jnp
from jax import lax
from jax.experimental import pallas as pl
from jax.experimental.pallas import tpu as pltpu

F32 = jnp.float32
BF16 = jnp.bfloat16

LN_EPS = 1e-5
SSM_GROUP = 16
SSM_STATE = 64
GROUPS_PER_BLOCK = 8
SCAN_ROWS = 16
BLK_TABLE_ROWS = 8
POOL_WINDOWS = (2, 4, 8, 16)
POOL_HALO = 16
N_XHEADS = 4
N_EXPERTS = 8
LANES = 128

VMEM_LIMIT_BYTES = 56 * 1024 * 1024

TM_PROJ = 512
TS_SCAN = 256
SCAN_LEVELS = (TS_SCAN // SCAN_ROWS).bit_length() - 1
assert TS_SCAN == SCAN_ROWS << SCAN_LEVELS and SCAN_LEVELS <= BLK_TABLE_ROWS
TM_ATTN = 512
TM_FFN = 512
TM_ROUTER = 512
TM_EXPERT = 896
IDX_ROWS = 8
TM_COMBINE = 256


def _params(*sem):
    return pltpu.CompilerParams(dimension_semantics=sem, vmem_limit_bytes=VMEM_LIMIT_BYTES)


def _layer_norm(z, g, b):
    mu = jnp.mean(z, axis=-1, keepdims=True)
    zc = z - mu
    var = jnp.mean(zc * zc, axis=-1, keepdims=True)
    return zc * lax.rsqrt(var + LN_EPS) * g + b


def _sigmoid(v):
    return 1.0 / (1.0 + jnp.exp(-v))


def _bdot(a, b):
    return jnp.dot(a, b, preferred_element_type=F32)


def _ssm_prep_kernel(a_re_ref, a_im_ref, log_dt_ref, b_re_ref, b_im_ref,
                     pos_re_ref, pos_im_ref, neg_re_ref, neg_im_ref, blk_re_ref, blk_im_ref, bb_re_ref, bb_im_ref):
    a_re = a_re_ref[...]
    a_im = a_im_ref[...]
    dt = jnp.exp(log_dt_ref[...])
    mag = jnp.exp(a_re * dt)
    lam_re = mag * jnp.cos(a_im * dt)
    lam_im = mag * jnp.sin(a_im * dt)
    den = a_re * a_re + a_im * a_im
    num_re = lam_re - 1.0
    f_re = (num_re * a_re + lam_im * a_im) / den
    f_im = (lam_im * a_re - num_re * a_im) / den
    b_re = b_re_ref[...]
    b_im = b_im_ref[...]
    bb_re_ref[...] = f_re[:, None, :] * b_re - f_im[:, None, :] * b_im
    bb_im_ref[...] = f_re[:, None, :] * b_im + f_im[:, None, :] * b_re
    m2 = lam_re * lam_re + lam_im * lam_im
    inv_re = lam_re / m2
    inv_im = -lam_im / m2
    p_re, p_im = lam_re, lam_im
    n_re, n_im = inv_re, inv_im
    for k in range(SCAN_ROWS):
        pos_re_ref[k] = p_re
        pos_im_ref[k] = p_im
        neg_re_ref[k] = n_re
        neg_im_ref[k] = n_im
        if k + 1 < SCAN_ROWS:
            p_re, p_im = p_re * lam_re - p_im * lam_im, p_re * lam_im + p_im * lam_re
            n_re, n_im = n_re * inv_re - n_im * inv_im, n_re * inv_im + n_im * inv_re
    q_re, q_im = p_re, p_im
    for k in range(blk_re_ref.shape[0]):
        if k < SCAN_LEVELS:
            blk_re_ref[k] = q_re
            blk_im_ref[k] = q_im
            q_re, q_im = q_re * q_re - q_im * q_im, 2.0 * q_re * q_im
        else:
            blk_re_ref[k] = jnp.zeros_like(q_re)
            blk_im_ref[k] = jnp.zeros_like(q_im)


def _ssm_prep(a_re, a_im, log_dt, b_re, b_im):
    nl, g, p = a_re.shape
    c = b_re.shape[-1]
    lg = nl * g
    tab = jax.ShapeDtypeStruct((SCAN_ROWS, lg, p), F32)
    blk = jax.ShapeDtypeStruct((BLK_TABLE_ROWS, lg, p), F32)
    bbs = jax.ShapeDtypeStruct((lg, c, p), F32)
    outs = pl.pallas_call(
        _ssm_prep_kernel,
        out_shape=(tab,) * 4 + (blk, blk, bbs, bbs),
        name="ssm_prep",
    )(a_re.reshape(lg, p), a_im.reshape(lg, p), log_dt.reshape(lg, 1),
      b_re.transpose(0, 1, 3, 2).reshape(lg, c, p), b_im.transpose(0, 1, 3, 2).reshape(lg, c, p))

    def tab_layout(t):
        return t.reshape(t.shape[0], nl, g * p).transpose(1, 0, 2)

    return tuple(tab_layout(t) for t in outs[:6]) + (outs[6].reshape(nl, g, c, p), outs[7].reshape(nl, g, c, p))


def _block_diag_in(bb_re, bb_im):
    g, c, p = bb_re.shape
    nb = g // GROUPS_PER_BLOCK
    eye = jnp.eye(GROUPS_PER_BLOCK, dtype=F32)

    def one(bb):
        bb = bb.reshape(nb, GROUPS_PER_BLOCK, c, p)
        m = bb[:, :, :, None, :] * eye[None, :, None, :, None]
        return m.reshape(nb, GROUPS_PER_BLOCK * c, GROUPS_PER_BLOCK * p)

    return jnp.concatenate([one(bb_re), one(bb_im)], axis=-1).astype(BF16)


def _block_diag_out(c_re, c_im):
    g, c, p = c_re.shape
    nb = g // GROUPS_PER_BLOCK
    eye = jnp.eye(GROUPS_PER_BLOCK, dtype=F32)

    def one(cm):
        cm = cm.reshape(nb, GROUPS_PER_BLOCK, c, p).transpose(0, 1, 3, 2)
        m = cm[:, :, :, None, :] * eye[None, :, None, :, None]
        return m.reshape(nb, GROUPS_PER_BLOCK * p, GROUPS_PER_BLOCK * c)

    return jnp.concatenate([one(c_re), -one(c_im)], axis=1).astype(BF16)


def _mix_in_kernel(x_ref, w_ref, us_ref, up_ref, g_ref, *, ssm_w, pool_w):
    xb = x_ref[...].astype(BF16)
    us_ref[...] = _bdot(xb, w_ref[:, :ssm_w])
    up_ref[...] = _bdot(xb, w_ref[:, ssm_w:ssm_w + pool_w])
    g_ref[...] = _sigmoid(_bdot(xb, w_ref[:, ssm_w + pool_w:])).astype(BF16)


def _mix_in(x, w_in, ssm_w, pool_w):
    t, d = x.shape
    cols = w_in.shape[1]
    gate_w = cols - ssm_w - pool_w
    tm = TM_PROJ
    return pl.pallas_call(
        functools.partial(_mix_in_kernel, ssm_w=ssm_w, pool_w=pool_w),
        grid=(t // tm,),
        in_specs=[pl.BlockSpec((tm, d), lambda i: (i, 0)),
                  pl.BlockSpec((d, cols), lambda i: (0, 0))],
        out_specs=[pl.BlockSpec((tm, ssm_w), lambda i: (i, 0)),
                   pl.BlockSpec((tm, pool_w), lambda i: (i, 0)),
                   pl.BlockSpec((tm, gate_w), lambda i: (i, 0))],
        out_shape=[jax.ShapeDtypeStruct((t, ssm_w), F32),
                   jax.ShapeDtypeStruct((t, pool_w), F32),
                   jax.ShapeDtypeStruct((t, gate_w), BF16)],
        compiler_params=_params("parallel"),
        name="mix_in",
    )(x, w_in)


def _mix_mid_kernel(us_ref, up_ref, bin_ref, cout_ref, pos_re_ref, pos_im_ref, neg_re_ref, neg_im_ref,
                    blk_re_ref, blk_im_ref, tri_ref, d_ref, wglu_ref, bglu_ref, pw_ref, ps_ref,
                    ys_ref, yp_ref, w_ref, halo_ref):
    ti = pl.program_id(1)
    ts = us_ref.shape[0]
    n_blocks = bin_ref.shape[0]
    half = bin_ref.shape[2] // 2
    ch = bin_ref.shape[1]
    nb = ts // SCAN_ROWS

    @pl.when(ti == 0)
    def _():
        w_ref[:, 0:SCAN_ROWS, :] = jnp.zeros((w_ref.shape[0], SCAN_ROWS, LANES), F32)
        halo_ref[...] = jnp.zeros_like(halo_ref)

    u = us_ref[...]
    ub = u.astype(BF16)
    blk_row = lax.broadcasted_iota(jnp.int32, (nb, half), 0)
    ys = []
    for j in range(n_blocks):
        c0 = j * 2 * half
        sl = slice(j * half, (j + 1) * half)
        bu = _bdot(ub[:, j * ch:(j + 1) * ch], bin_ref[j]).astype(BF16)
        bu_re = bu[:, :half].reshape(nb, SCAN_ROWS, half)
        bu_im = bu[:, half:].reshape(nb, SCAN_ROWS, half)
        n_re = neg_re_ref[:, sl][None]
        n_im = neg_im_ref[:, sl][None]
        z_re = (n_re * bu_re - n_im * bu_im).reshape(ts, half)
        z_im = (n_re * bu_im + n_im * bu_re).reshape(ts, half)
        z = jnp.concatenate([z_re, z_im], axis=1)
        w = _bdot(tri_ref[...], z)
        n_ch = 2 * half // LANES
        for c in range(n_ch):
            w_ref[j * n_ch + c, SCAN_ROWS:, :] = w[:, c * LANES:(c + 1) * LANES]
        e = [w_ref[j * n_ch + c, pl.ds(SCAN_ROWS - 1, nb, stride=SCAN_ROWS), :] for c in range(n_ch)]
        e_re = jnp.concatenate(e[:n_ch // 2], axis=1)
        e_im = jnp.concatenate(e[n_ch // 2:], axis=1)
        a_re = blk_re_ref[0:1, sl]
        a_im = blk_im_ref[0:1, sl]
        c_re = a_re * e_re - a_im * e_im
        c_im = a_re * e_im + a_im * e_re
        for k in range(SCAN_LEVELS):
            sh = 1 << k
            q_re = blk_re_ref[k:k + 1, sl]
            q_im = blk_im_ref[k:k + 1, sl]
            r_re = jnp.where(blk_row >= sh, pltpu.roll(c_re, sh, 0), 0.0)
            r_im = jnp.where(blk_row >= sh, pltpu.roll(c_im, sh, 0), 0.0)
            c_re, c_im = c_re + (q_re * r_re - q_im * r_im), c_im + (q_re * r_im + q_im * r_re)
        w_re = w[:, :half].reshape(nb, SCAN_ROWS, half)
        w_im = w[:, half:].reshape(nb, SCAN_ROWS, half)
        s_re = w_re + c_re[:, None, :]
        s_im = w_im + c_im[:, None, :]
        last = jnp.concatenate([s_re[nb - 1, SCAN_ROWS - 1:, :], s_im[nb - 1, SCAN_ROWS - 1:, :]], axis=1)
        for c in range(n_ch):
            w_ref[j * n_ch + c, SCAN_ROWS - 1:SCAN_ROWS, :] = last[:, c * LANES:(c + 1) * LANES]
        p_re = pos_re_ref[:, sl][None]
        p_im = pos_im_ref[:, sl][None]
        s_re = s_re.astype(BF16)
        s_im = s_im.astype(BF16)
        h_re = (p_re * s_re - p_im * s_im).reshape(ts, half)
        h_im = (p_re * s_im + p_im * s_re).reshape(ts, half)
        h = jnp.concatenate([h_re, h_im], axis=1)
        ys.append(_bdot(h, cout_ref[j]))
    y = jnp.concatenate(ys, axis=1) + d_ref[...] * u
    y = jax.nn.gelu(y)
    y = y * _sigmoid(_bdot(y.astype(BF16), wglu_ref[...]) + bglu_ref[...])
    ys_ref[...] = y.astype(ys_ref.dtype)

    up = up_ref[...]
    ext = jnp.concatenate([halo_ref[...], up], axis=0)
    halo_ref[...] = up[ts - POOL_HALO:, :]
    pg = pw_ref.shape[1]
    t_pos = (ti * ts + lax.broadcasted_iota(jnp.int32, (ts, 1), 0) + 1).astype(F32)
    outs = []
    for gi, win in enumerate(POOL_WINDOWS):
        e = ext[:, gi * pg:(gi + 1) * pg]
        s = e
        sh = 1
        while sh < win:
            s = s + pltpu.roll(s, sh, 0)
            sh *= 2
        mean = s[POOL_HALO:, :] / jnp.minimum(t_pos, float(win))
        dlt = (mean - e[POOL_HALO:, :]).astype(BF16)
        outs.append(_bdot(dlt, pw_ref[gi]))
    yp = jnp.concatenate(outs, axis=1) * ps_ref[...]
    yp_ref[...] = yp.astype(yp_ref.dtype)


def _mix_mid(us, up, batch, bin_blk, cout_blk, tables, tri, ssm_d, w_glu, b_glu, pool_w, pool_scale):
    t, sw = us.shape
    pw = up.shape[1]
    seq = t // batch
    ts = TS_SCAN
    nt = seq // ts
    n_state = tables[0].shape[1]

    def full(a):
        return pl.BlockSpec(a.shape, lambda b, i, n=a.ndim: (0,) * n)

    row = lambda b, i: (b * nt + i, 0)
    consts = (bin_blk, cout_blk) + tuple(tables) + (tri, ssm_d, w_glu, b_glu, pool_w, pool_scale)
    return pl.pallas_call(
        _mix_mid_kernel,
        grid=(batch, nt),
        in_specs=[pl.BlockSpec((ts, sw), row), pl.BlockSpec((ts, pw), row)] + [full(a) for a in consts],
        out_specs=[pl.BlockSpec((ts, sw), row), pl.BlockSpec((ts, pw), row)],
        out_shape=[jax.ShapeDtypeStruct((t, sw), BF16), jax.ShapeDtypeStruct((t, pw), BF16)],
        scratch_shapes=[pltpu.VMEM((2 * n_state // LANES, SCAN_ROWS + ts, LANES), F32),
                        pltpu.VMEM((POOL_HALO, pw), F32)],
        compiler_params=_params("parallel", "arbitrary"),
        name="mix_mid",
    )(us, up, *consts)


def _mix_out_kernel(x_ref, ys_ref, yp_ref, g_ref, wsu_ref, wpu_ref, wo_ref, lg_ref, lb_ref, o_ref, *, alpha):
    d = x_ref.shape[1]
    y_ssm = _bdot(ys_ref[...], wsu_ref[...])
    y_pool = _bdot(yp_ref[...], wpu_ref[...])
    g = g_ref[...].astype(F32)
    comb = g[:, :d] * y_ssm + g[:, d:] * y_pool
    y = _bdot(comb.astype(BF16), wo_ref[...])
    o_ref[...] = _layer_norm(alpha * x_ref[...] + y, lg_ref[...], lb_ref[...])


def _mix_out(x, ys, yp, gates, w_ssm_up, w_pool_up, w_out, ln_g, ln_b, alpha):
    t, d = x.shape
    tm = TM_PROJ
    row = lambda i: (i, 0)
    full = lambda a: pl.BlockSpec(a.shape, lambda i, n=a.ndim: (0,) * n)
    return pl.pallas_call(
        functools.partial(_mix_out_kernel, alpha=alpha),
        grid=(t // tm,),
        in_specs=[pl.BlockSpec((tm, d), row), pl.BlockSpec((tm, ys.shape[1]), row),
                  pl.BlockSpec((tm, yp.shape[1]), row), pl.BlockSpec((tm, gates.shape[1]), row),
                  full(w_ssm_up), full(w_pool_up), full(w_out), full(ln_g), full(ln_b)],
        out_specs=pl.BlockSpec((tm, d), row),
        out_shape=jax.ShapeDtypeStruct((t, d), F32),
        compiler_params=_params("parallel"),
        name="mix_out",
    )(x, ys, yp, gates, w_ssm_up, w_pool_up, w_out, ln_g, ln_b)


def _kv_kernel(m_ref, wk_ref, wv_ref, k_ref, v_ref):
    mb = m_ref[...].astype(BF16)
    k_ref[...] = _bdot(mb, wk_ref[...]).astype(BF16)
    v_ref[...] = _bdot(mb, wv_ref[...]).astype(BF16)


def _kv_proj(mem2d, wk, wv):
    r, d = mem2d.shape
    tm = 256
    full = lambda a: pl.BlockSpec(a.shape, lambda i: (0, 0))
    return pl.pallas_call(
        _kv_kernel,
        grid=(r // tm,),
        in_specs=[pl.BlockSpec((tm, d), lambda i: (i, 0)), full(wk), full(wv)],
        out_specs=[pl.BlockSpec((tm, d), lambda i: (i, 0))] * 2,
        out_shape=[jax.ShapeDtypeStruct((r, d), BF16)] * 2,
        compiler_params=_params("parallel"),
        name="xattn_kv",
    )(mem2d, wk, wv)


def _xattn_kernel(x_ref, k_ref, v_ref, wq_ref, wo_ref, lg_ref, lb_ref, o_ref, *, alpha):
    x = x_ref[...]
    d = x.shape[1]
    hd = d // N_XHEADS
    q = _bdot(x.astype(BF16), wq_ref[...]) * (hd ** -0.5)
    heads = []
    for h in range(N_XHEADS):
        qh = q[:, h * hd:(h + 1) * hd].astype(BF16)
        kh = k_ref[:, h * hd:(h + 1) * hd]
        s = lax.dot_general(qh, kh, (((1,), (1,)), ((), ())), preferred_element_type=F32)
        s = s - jnp.max(s, axis=-1, keepdims=True)
        p = jnp.exp(s)
        p = p / jnp.sum(p, axis=-1, keepdims=True)
        heads.append(_bdot(p.astype(BF16), v_ref[:, h * hd:(h + 1) * hd]))
    att = jnp.concatenate(heads, axis=1).astype(BF16)
    y = _bdot(att, wo_ref[...])
    o_ref[...] = _layer_norm(alpha * x + y, lg_ref[...], lb_ref[...])


def _xattn(x, k, v, batch, wq, wo, ln_g, ln_b, alpha):
    t, d = x.shape
    n_mem = k.shape[0] // batch
    tm = TM_ATTN
    nt = t // batch // tm
    full = lambda a: pl.BlockSpec(a.shape, lambda b, i, n=a.ndim: (0,) * n)
    row = lambda b, i: (b * nt + i, 0)
    return pl.pallas_call(
        functools.partial(_xattn_kernel, alpha=alpha),
        grid=(batch, nt),
        in_specs=[pl.BlockSpec((tm, d), row),
                  pl.BlockSpec((n_mem, d), lambda b, i: (b, 0)), pl.BlockSpec((n_mem, d), lambda b, i: (b, 0)),
                  full(wq), full(wo), full(ln_g), full(ln_b)],
        out_specs=pl.BlockSpec((tm, d), row),
        out_shape=jax.ShapeDtypeStruct((t, d), F32),
        compiler_params=_params("parallel", "parallel"),
        name="xattn",
    )(x, k, v, wq, wo, ln_g, ln_b)


def _ffn_kernel(x_ref, w1_ref, w3_ref, w2_ref, lg_ref, lb_ref, o_ref, acc_ref, *, alpha):
    f = pl.program_id(1)
    xb = x_ref[...].astype(BF16)
    a = _bdot(xb, w1_ref[...])
    h = (a * _sigmoid(a) * _bdot(xb, w3_ref[...])).astype(BF16)
    part = _bdot(h, w2_ref[...])

    @pl.when(f == 0)
    def _():
        acc_ref[...] = part

    @pl.when(f > 0)
    def _():
        acc_ref[...] += part

    @pl.when(f == pl.num_programs(1) - 1)
    def _():
        o_ref[...] = _layer_norm(alpha * x_ref[...] + acc_ref[...], lg_ref[...], lb_ref[...])


def _ffn_tile(ff):
    for cand in (512, 1408, 1024, 768, 256, 128):
        if ff % cand == 0:
            return cand
    return ff


def _ffn(x, w1, w3, w2, ln_g, ln_b, alpha):
    t, d = x.shape
    ff = w1.shape[1]
    tm, tf = TM_FFN, _ffn_tile(ff)
    full = lambda a: pl.BlockSpec(a.shape, lambda i, f, n=a.ndim: (0,) * n)
    return pl.pallas_call(
        functools.partial(_ffn_kernel, alpha=alpha),
        grid=(t // tm, ff // tf),
        in_specs=[pl.BlockSpec((tm, d), lambda i, f: (i, 0)),
                  pl.BlockSpec((d, tf), lambda i, f: (0, f)), pl.BlockSpec((d, tf), lambda i, f: (0, f)),
                  pl.BlockSpec((tf, d), lambda i, f: (f, 0)), full(ln_g), full(ln_b)],
        out_specs=pl.BlockSpec((tm, d), lambda i, f: (i, 0)),
        out_shape=jax.ShapeDtypeStruct((t, d), F32),
        scratch_shapes=[pltpu.VMEM((tm, d), F32)],
        compiler_params=_params("parallel", "arbitrary"),
        name="ffn_dense",
    )(x, w1, w3, w2, ln_g, ln_b)


def _router_kernel(x_ref, r_ref, o_ref):
    logits = jnp.dot(x_ref[...], r_ref[...], preferred_element_type=F32, precision=lax.Precision.HIGHEST)
    lane = lax.broadcasted_iota(jnp.int32, logits.shape, 1)
    lane_f = lane.astype(F32)
    neg = jnp.float32(-jnp.inf)
    lg = jnp.where(lane < N_EXPERTS, logits, neg)
    m1 = jnp.max(lg, axis=-1, keepdims=True)
    i1 = jnp.min(jnp.where(lg == m1, lane_f, float(LANES)), axis=-1, keepdims=True)
    lg2 = jnp.where(lane_f == i1, neg, lg)
    m2 = jnp.max(lg2, axis=-1, keepdims=True)
    i2 = jnp.min(jnp.where(lg2 == m2, lane_f, float(LANES)), axis=-1, keepdims=True)
    e = jnp.exp(m2 - m1)
    w1 = 1.0 / (1.0 + e)
    w2 = e / (1.0 + e)
    out = jnp.where(lane == 0, w1, jnp.where(lane == 1, w2, jnp.where(lane == 2, i1, jnp.where(lane == 3, i2, 0.0))))
    o_ref[...] = out


def _router(x, router_pad):
    t, d = x.shape
    tm = TM_ROUTER
    return pl.pallas_call(
        _router_kernel,
        grid=(t // tm,),
        in_specs=[pl.BlockSpec((tm, d), lambda i: (i, 0)), pl.BlockSpec(router_pad.shape, lambda i: (0, 0))],
        out_specs=pl.BlockSpec((tm, LANES), lambda i: (i, 0)),
        out_shape=jax.ShapeDtypeStruct((t, LANES), F32),
        compiler_params=_params("parallel"),
        name="moe_router",
    )(x, router_pad)


def _expert_kernel(te_ref, src_hbm, dst_hbm, x_hbm, w1_ref, w3_ref, w2_ref, y_hbm,
                   src_smem, dst_smem, xbuf_ref, xb_ref, obuf_ref, src_sem, dst_sem, gat_sem, sct_sem):
    i = pl.program_id(0)
    f = pl.program_id(1)
    n_tiles = pl.num_programs(0)
    _, n_f, chunk, d = xbuf_ref.shape
    tm = n_f * chunk
    slot = i % 2
    nxt = (i + 1) % 2

    def idx_rows(s):
        return pl.ds(pl.multiple_of(s * IDX_ROWS, IDX_ROWS), IDX_ROWS)

    def src_copy(tile, s):
        return pltpu.make_async_copy(src_hbm.at[tile], src_smem.at[idx_rows(s)], src_sem)

    def dst_copy(tile, s):
        return pltpu.make_async_copy(dst_hbm.at[tile + 1], dst_smem.at[idx_rows(s)], dst_sem)

    def gather_row(s, c, k):
        tok = src_smem[s * IDX_ROWS + c, k]
        return pltpu.make_async_copy(x_hbm.at[pl.ds(tok, 1)], xbuf_ref.at[s, c, pl.ds(k, 1)], gat_sem.at[s])

    def scatter_row(s, c, k):
        row = dst_smem[s * IDX_ROWS + c, k]
        return pltpu.make_async_copy(obuf_ref.at[s, c, pl.ds(k, 1)], y_hbm.at[pl.ds(row, 1)], sct_sem.at[s])

    def wait_gather(s):
        for c in range(n_f):
            pltpu.make_async_copy(x_hbm.at[pl.ds(0, chunk)], xbuf_ref.at[s, c], gat_sem.at[s]).wait()

    def wait_scatter(s):
        for c in range(n_f):
            pltpu.make_async_copy(obuf_ref.at[s, c], y_hbm.at[pl.ds(0, chunk)], sct_sem.at[s]).wait()

    @pl.when(f == 0)
    def _():
        @pl.when(i == 0)
        def _():
            for cp in (src_copy(0, 0), src_copy(1, 1), dst_copy(-1, 1)):
                cp.start()
                cp.wait()
            obuf_ref[...] = jnp.zeros(obuf_ref.shape, obuf_ref.dtype)
            for c in range(n_f):
                def body(k, carry, c=c):
                    gather_row(0, c, k).start()
                    return carry
                lax.fori_loop(0, chunk, body, 0, unroll=8)

        @pl.when(i > 0)
        def _():
            src_copy(i + 1, nxt).wait()
            dst_copy(i - 1, nxt).wait()
            wait_scatter(slot)

        src_copy(i + 2, slot).start()
        dst_copy(i, slot).start()
        wait_gather(slot)
        xb_ref[...] = xbuf_ref[slot].reshape(tm, d).astype(BF16)

    for k in range(chunk):
        gather_row(nxt, f, k).start()
        scatter_row(nxt, f, k).start(priority=k % 2)

    xb = xb_ref[...]
    a = _bdot(xb, w1_ref[...].astype(BF16))
    h = (a * _sigmoid(a) * _bdot(xb, w3_ref[...].astype(BF16))).astype(BF16)
    part = _bdot(h, w2_ref[...].astype(BF16))

    prev = jnp.where(f > 0, obuf_ref[slot].reshape(tm, d), 0.0)
    obuf_ref[slot] = (prev + part).reshape(n_f, chunk, d)

    @pl.when(f == n_f - 1)
    def _():
        @pl.when(i == n_tiles - 1)
        def _():
            wait_scatter(nxt)
            dst_copy(i, slot).wait()
            for c in range(n_f):
                def body(k, carry, c=c):
                    scatter_row(slot, c, k).start()
                    return carry
                lax.fori_loop(0, chunk, body, 0, unroll=8)
            wait_scatter(slot)
            wait_gather(nxt)
            src_copy(i + 2, slot).wait()


def _experts(x, src3d, dst3d, tile_expert, w1, w3, w2):
    n_tiles = tile_expert.shape[0]
    chunk = src3d.shape[2]
    d = x.shape[1]
    ff = w1.shape[2]
    tf = _ffn_tile(ff)
    n_f = ff // tf
    tm = n_f * chunk
    assert 2 <= n_f <= IDX_ROWS and chunk % 8 == 0
    assert src3d.shape == (n_tiles + 2, IDX_ROWS, chunk) and dst3d.shape == (n_tiles + 1, IDX_ROWS, chunk)
    any_spec = pl.BlockSpec(memory_space=pl.ANY)
    grid_spec = pltpu.PrefetchScalarGridSpec(
        num_scalar_prefetch=1,
        grid=(n_tiles, n_f),
        in_specs=[any_spec, any_spec, any_spec,
                  pl.BlockSpec((None, d, tf), lambda i, f, te: (te[i], 0, f)),
                  pl.BlockSpec((None, d, tf), lambda i, f, te: (te[i], 0, f)),
                  pl.BlockSpec((None, tf, d), lambda i, f, te: (te[i], f, 0))],
        out_specs=any_spec,
        scratch_shapes=[pltpu.SMEM((2 * IDX_ROWS, chunk), jnp.int32), pltpu.SMEM((2 * IDX_ROWS, chunk), jnp.int32),
                        pltpu.VMEM((2, n_f, chunk, d), F32),
                        pltpu.VMEM((tm, d), BF16),
                        pltpu.VMEM((2, n_f, chunk, d), F32),
                        pltpu.SemaphoreType.DMA, pltpu.SemaphoreType.DMA,
                        pltpu.SemaphoreType.DMA((2,)), pltpu.SemaphoreType.DMA((2,))],
    )
    return pl.pallas_call(
        _expert_kernel,
        grid_spec=grid_spec,
        out_shape=jax.ShapeDtypeStruct(((n_tiles + 1) * tm, d), F32),
        compiler_params=_params("arbitrary", "arbitrary"),
        name="moe_experts",
    )(tile_expert, src3d, dst3d, x, w1, w3, w2)


def _combine_kernel(info_ref, x_ref, y0_ref, y1_ref, lg_ref, lb_ref, o_ref, *, alpha):
    info = info_ref[...]
    y = info[:, 0:1] * y0_ref[...] + info[:, 1:2] * y1_ref[...]
    o_ref[...] = _layer_norm(alpha * x_ref[...] + y, lg_ref[...], lb_ref[...])


def _combine(info, x, y, ln_g, ln_b, alpha):
    t, d = x.shape
    tm = TM_COMBINE
    nt = t // tm
    full = lambda a: pl.BlockSpec(a.shape, lambda i, n=a.ndim: (0,) * n)
    return pl.pallas_call(
        functools.partial(_combine_kernel, alpha=alpha),
        grid=(nt,),
        in_specs=[pl.BlockSpec((tm, LANES), lambda i: (i, 0)), pl.BlockSpec((tm, d), lambda i: (i, 0)),
                  pl.BlockSpec((tm, d), lambda i: (i, 0)), pl.BlockSpec((tm, d), lambda i: (i + nt, 0)),
                  full(ln_g), full(ln_b)],
        out_specs=pl.BlockSpec((tm, d), lambda i: (i, 0)),
        out_shape=jax.ShapeDtypeStruct((t, d), F32),
        compiler_params=_params("parallel"),
        name="moe_combine",
    )(info, x, y, y, ln_g, ln_b)


def _moe(x, router, w1, w3, w2, ln_g, ln_b, alpha):
    t, d = x.shape
    ne = router.shape[1]
    router_pad = jnp.pad(router, ((0, 0), (0, LANES - ne)))
    info = _router(x, router_pad)
    top_idx = info[:, 2:4].astype(jnp.int32).reshape(-1)
    onehot = (top_idx[:, None] == jnp.arange(ne, dtype=jnp.int32)[None, :]).astype(jnp.int32)
    rank = jnp.sum((jnp.cumsum(onehot, axis=0) - onehot) * onehot, axis=1)
    counts = jnp.sum(onehot, axis=0)
    padded = ((counts + TM_EXPERT - 1) // TM_EXPERT) * TM_EXPERT
    ends = jnp.cumsum(padded)
    offs = ends - padded
    pos = offs[top_idx] + rank
    n_tiles = (2 * t + ne * (TM_EXPERT - 1) + TM_EXPERT - 1) // TM_EXPERT
    n_rows = n_tiles * TM_EXPERT
    assign = jnp.full((n_rows,), -1, jnp.int32).at[pos].set(jnp.arange(2 * t, dtype=jnp.int32))
    real = assign >= 0
    pad_rank = jnp.cumsum(1 - real.astype(jnp.int32)) - 1
    src = jnp.where(real, assign // 2, 0)
    dst = jnp.where(real, (assign % 2) * t + assign // 2, 2 * t + pad_rank)
    tile_start = jnp.arange(n_tiles, dtype=jnp.int32) * TM_EXPERT
    tile_used = tile_start < ends[-1]
    tile_expert = jnp.sum((tile_start[:, None] >= ends[None, :]).astype(jnp.int32), axis=1)
    last_used = jnp.maximum(ends[-1] // TM_EXPERT - 1, 0)
    tile_expert = jnp.where(tile_used, jnp.minimum(tile_expert, ne - 1), tile_expert[last_used])
    n_f = w1.shape[2] // _ffn_tile(w1.shape[2])

    def idx_blocks(a):
        a = a.reshape(-1, n_f, TM_EXPERT // n_f)
        return jnp.pad(a, ((0, 0), (0, IDX_ROWS - n_f), (0, 0)))

    spare = n_rows + jnp.arange(TM_EXPERT, dtype=jnp.int32)
    src3d = idx_blocks(jnp.concatenate([src, jnp.zeros((2 * TM_EXPERT,), jnp.int32)]))
    dst3d = idx_blocks(jnp.concatenate([spare, dst]))

    y = _experts(x, src3d, dst3d, tile_expert, w1, w3, w2)
    return _combine(info, x, y, ln_g, ln_b, alpha)


def kernel(x, mem, ln_mix_g, ln_mix_b, w_in, ssm_a_re, ssm_a_im, ssm_log_dt, ssm_b_re, ssm_b_im, ssm_c_re, ssm_c_im, ssm_d, ssm_w_glu, ssm_b_glu, w_ssm_up, pool_w, pool_scale, w_pool_up, w_out, ln_xa_g, ln_xa_b, xa_wq, xa_wk, xa_wv, xa_wo, ln_ffn_g, ln_ffn_b, ffn_w1, ffn_w3, ffn_w2, moe_router, moe_w1, moe_w3, moe_w2):
    batch, seq, d = x.shape
    depth = w_in.shape[0]
    alpha = (2 * depth) ** 0.25
    ssm_w = ssm_d.shape[1]
    pool_width = pool_scale.shape[1]
    t = batch * seq
    assert seq % TS_SCAN == 0 and t % TM_PROJ == 0 and TS_SCAN % SCAN_ROWS == 0

    *ssm_tables, bb_re, bb_im = _ssm_prep(ssm_a_re, ssm_a_im, ssm_log_dt, ssm_b_re, ssm_b_im)
    r = jnp.arange(TS_SCAN)
    tri = ((r[:, None] // SCAN_ROWS == r[None, :] // SCAN_ROWS) & (r[None, :] <= r[:, None])).astype(BF16)
    bf = lambda a: a.astype(BF16)
    row = lambda a: a.reshape(1, -1)

    xf = x.reshape(t, d)
    mem2d = mem.reshape(-1, d)
    for l in range(depth):
        us, up, gates = _mix_in(xf, bf(w_in[l]), ssm_w, pool_width)
        ys, yp = _mix_mid(us, up, batch, _block_diag_in(bb_re[l], bb_im[l]),
                          _block_diag_out(ssm_c_re[l], ssm_c_im[l]),
                          [bf(tab[l]) for tab in ssm_tables[:4]] + [tab[l] for tab in ssm_tables[4:]],
                          tri, row(ssm_d[l]), bf(ssm_w_glu[l]),
                          row(ssm_b_glu[l]), bf(pool_w[l]), row(pool_scale[l]))
        xf = _mix_out(xf, ys, yp, gates, bf(w_ssm_up[l]), bf(w_pool_up[l]), bf(w_out[l]),
                      row(ln_mix_g[l]), row(ln_mix_b[l]), alpha)
        k, v = _kv_proj(mem2d, bf(xa_wk[l]), bf(xa_wv[l]))
        xf = _xattn(xf, k, v, batch, bf(xa_wq[l]), bf(xa_wo[l]), row(ln_xa_g[l]), row(ln_xa_b[l]), alpha)
        if l % 2 == 0:
            i = l // 2
            xf = _ffn(xf, bf(ffn_w1[i]), bf(ffn_w3[i]), bf(ffn_w2[i]), row(ln_ffn_g[l]), row(ln_ffn_b[l]), alpha)
        else:
            i = l // 2
            xf = _moe(xf, moe_router[i], moe_w1[i], moe_w3[i], moe_w2[i],
                      row(ln_ffn_g[l]), row(ln_ffn_b[l]), alpha)
    return xf.reshape(batch, seq, d)
```

```python
import functools
import math

import jax
import jax.numpy as jnp
from jax import lax
from jax.experimental import pallas as pl
from jax.experimental.pallas import tpu as pltpu

F32 = jnp.float32
BF16 = jnp.bfloat16

LN_EPS = 1e-5
SSM_GROUP = 16
SSM_STATE = 64
GROUPS_PER_BLOCK = 8
SCAN_ROWS = 16
BLK_TABLE_ROWS = 8
POOL_WINDOWS = (2, 4, 8, 16)
POOL_HALO = 16
N_XHEADS = 4
N_EXPERTS = 8
LANES = 128

VMEM_LIMIT_BYTES = 56 * 1024 * 1024

TM_PROJ = 512
TS_SCAN = 256
SCAN_LEVELS = (TS_SCAN // SCAN_ROWS).bit_length() - 1
assert TS_SCAN == SCAN_ROWS << SCAN_LEVELS and SCAN_LEVELS <= BLK_TABLE_ROWS
TM_ATTN = 512
TM_FFN = 512
TM_ROUTER = 512
TM_EXPERT = 896
IDX_ROWS = 8
TM_COMBINE = 256


def _params(*sem):
    return pltpu.CompilerParams(dimension_semantics=sem, vmem_limit_bytes=VMEM_LIMIT_BYTES)


def _layer_norm(z, g, b):
    mu = jnp.mean(z, axis=-1, keepdims=True)
    zc = z - mu
    var = jnp.mean(zc * zc, axis=-1, keepdims=True)
    return zc * lax.rsqrt(var + LN_EPS) * g + b


def _sigmoid(v):
    return 1.0 / (1.0 + jnp.exp(-v))


def _bdot(a, b):
    return jnp.dot(a, b, preferred_element_type=F32)


def _ssm_prep_kernel(a_re_ref, a_im_ref, log_dt_ref, b_re_ref, b_im_ref,
                     pos_re_ref, pos_im_ref, neg_re_ref, neg_im_ref, blk_re_ref, blk_im_ref, bb_re_ref, bb_im_ref):
    a_re = a_re_ref[...]
    a_im = a_im_ref[...]
    dt = jnp.exp(log_dt_ref[...])
    mag = jnp.exp(a_re * dt)
    lam_re = mag * jnp.cos(a_im * dt)
    lam_im = mag * jnp.sin(a_im * dt)
    den = a_re * a_re + a_im * a_im
    num_re = lam_re - 1.0
    f_re = (num_re * a_re + lam_im * a_im) / den
    f_im = (lam_im * a_re - num_re * a_im) / den
    b_re = b_re_ref[...]
    b_im = b_im_ref[...]
    bb_re_ref[...] = f_re[:, None, :] * b_re - f_im[:, None, :] * b_im
    bb_im_ref[...] = f_re[:, None, :] * b_im + f_im[:, None, :] * b_re
    m2 = lam_re * lam_re + lam_im * lam_im
    inv_re = lam_re / m2
    inv_im = -lam_im / m2
    p_re, p_im = lam_re, lam_im
    n_re, n_im = inv_re, inv_im
    for k in range(SCAN_ROWS):
        pos_re_ref[k] = p_re
        pos_im_ref[k] = p_im
        neg_re_ref[k] = n_re
        neg_im_ref[k] = n_im
        if k + 1 < SCAN_ROWS:
            p_re, p_im = p_re * lam_re - p_im * lam_im, p_re * lam_im + p_im * lam_re
            n_re, n_im = n_re * inv_re - n_im * inv_im, n_re * inv_im + n_im * inv_re
    q_re, q_im = p_re, p_im
    for k in range(blk_re_ref.shape[0]):
        if k < SCAN_LEVELS:
            blk_re_ref[k] = q_re
            blk_im_ref[k] = q_im
            q_re, q_im = q_re * q_re - q_im * q_im, 2.0 * q_re * q_im
        else:
            blk_re_ref[k] = jnp.zeros_like(q_re)
            blk_im_ref[k] = jnp.zeros_like(q_im)


def _ssm_prep(a_re, a_im, log_dt, b_re, b_im):
    nl, g, p = a_re.shape
    c = b_re.shape[-1]
    lg = nl * g
    tab = jax.ShapeDtypeStruct((SCAN_ROWS, lg, p), F32)
    blk = jax.ShapeDtypeStruct((BLK_TABLE_ROWS, lg, p), F32)
    bbs = jax.ShapeDtypeStruct((lg, c, p), F32)
    outs = pl.pallas_call(
        _ssm_prep_kernel,
        out_shape=(tab,) * 4 + (blk, blk, bbs, bbs),
        name="ssm_prep",
    )(a_re.reshape(lg, p), a_im.reshape(lg, p), log_dt.reshape(lg, 1),
      b_re.transpose(0, 1, 3, 2).reshape(lg, c, p), b_im.transpose(0, 1, 3, 2).reshape(lg, c, p))

    def tab_layout(t):
        return t.reshape(t.shape[0], nl, g * p).transpose(1, 0, 2)

    return tuple(tab_layout(t) for t in outs[:6]) + (outs[6].reshape(nl, g, c, p), outs[7].reshape(nl, g, c, p))


def _block_diag_in(bb_re, bb_im):
    g, c, p = bb_re.shape
    nb = g // GROUPS_PER_BLOCK
    eye = jnp.eye(GROUPS_PER_BLOCK, dtype=F32)

    def one(bb):
        bb = bb.reshape(nb, GROUPS_PER_BLOCK, c, p)
        m = bb[:, :, :, None, :] * eye[None, :, None, :, None]
        return m.reshape(nb, GROUPS_PER_BLOCK * c, GROUPS_PER_BLOCK * p)

    return jnp.concatenate([one(bb_re), one(bb_im)], axis=-1).astype(BF16)


def _block_diag_out(c_re, c_im):
    g, c, p = c_re.shape
    nb = g // GROUPS_PER_BLOCK
    eye = jnp.eye(GROUPS_PER_BLOCK, dtype=F32)

    def one(cm):
        cm = cm.reshape(nb, GROUPS_PER_BLOCK, c, p).transpose(0, 1, 3, 2)
        m = cm[:, :, :, None, :] * eye[None, :, None, :, None]
        return m.reshape(nb, GROUPS_PER_BLOCK * p, GROUPS_PER_BLOCK * c)

    return jnp.concatenate([one(c_re), -one(c_im)], axis=1).astype(BF16)


def _mix_in_kernel(x_ref, w_ref, us_ref, up_ref, g_ref, *, ssm_w, pool_w):
    xb = x_ref[...].astype(BF16)
    us_ref[...] = _bdot(xb, w_ref[:, :ssm_w])
    up_ref[...] = _bdot(xb, w_ref[:, ssm_w:ssm_w + pool_w])
    g_ref[...] = _sigmoid(_bdot(xb, w_ref[:, ssm_w + pool_w:])).astype(BF16)


def _mix_in(x, w_in, ssm_w, pool_w):
    t, d = x.shape
    cols = w_in.shape[1]
    gate_w = cols - ssm_w - pool_w
    tm = TM_PROJ
    return pl.pallas_call(
        functools.partial(_mix_in_kernel, ssm_w=ssm_w, pool_w=pool_w),
        grid=(t // tm,),
        in_specs=[pl.BlockSpec((tm, d), lambda i: (i, 0)),
                  pl.BlockSpec((d, cols), lambda i: (0, 0))],
        out_specs=[pl.BlockSpec((tm, ssm_w), lambda i: (i, 0)),
                   pl.BlockSpec((tm, pool_w), lambda i: (i, 0)),
                   pl.BlockSpec((tm, gate_w), lambda i: (i, 0))],
        out_shape=[jax.ShapeDtypeStruct((t, ssm_w), F32),
                   jax.ShapeDtypeStruct((t, pool_w), F32),
                   jax.ShapeDtypeStruct((t, gate_w), BF16)],
        compiler_params=_params("parallel"),
        name="mix_in",
    )(x, w_in)


def _mix_mid_kernel(us_ref, up_ref, bin_ref, cout_ref, pos_re_ref, pos_im_ref, neg_re_ref, neg_im_ref,
                    blk_re_ref, blk_im_ref, tri_ref, d_ref, wglu_ref, bglu_ref, pw_ref, ps_ref,
                    ys_ref, yp_ref, w_ref, halo_ref):
    ti = pl.program_id(1)
    ts = us_ref.shape[0]
    n_blocks = bin_ref.shape[0]
    half = bin_ref.shape[2] // 2
    ch = bin_ref.shape[1]
    nb = ts // SCAN_ROWS

    @pl.when(ti == 0)
    def _():
        w_ref[:, 0:SCAN_ROWS, :] = jnp.zeros((w_ref.shape[0], SCAN_ROWS, LANES), F32)
        halo_ref[...] = jnp.zeros_like(halo_ref)

    u = us_ref[...]
    ub = u.astype(BF16)
    blk_row = lax.broadcasted_iota(jnp.int32, (nb, half), 0)
    ys = []
    for j in range(n_blocks):
        c0 = j * 2 * half
        sl = slice(j * half, (j + 1) * half)
        bu = _bdot(ub[:, j * ch:(j + 1) * ch], bin_ref[j]).astype(BF16)
        bu_re = bu[:, :half].reshape(nb, SCAN_ROWS, half)
        bu_im = bu[:, half:].reshape(nb, SCAN_ROWS, half)
        n_re = neg_re_ref[:, sl][None]
        n_im = neg_im_ref[:, sl][None]
        z_re = (n_re * bu_re - n_im * bu_im).reshape(ts, half)
        z_im = (n_re * bu_im + n_im * bu_re).reshape(ts, half)
        z = jnp.concatenate([z_re, z_im], axis=1)
        w = _bdot(tri_ref[...], z)
        n_ch = 2 * half // LANES
        for c in range(n_ch):
            w_ref[j * n_ch + c, SCAN_ROWS:, :] = w[:, c * LANES:(c + 1) * LANES]
        e = [w_ref[j * n_ch + c, pl.ds(SCAN_ROWS - 1, nb, stride=SCAN_ROWS), :] for c in range(n_ch)]
        e_re = jnp.concatenate(e[:n_ch // 2], axis=1)
        e_im = jnp.concatenate(e[n_ch // 2:], axis=1)
        a_re = blk_re_ref[0:1, sl]
        a_im = blk_im_ref[0:1, sl]
        c_re = a_re * e_re - a_im * e_im
        c_im = a_re * e_im + a_im * e_re
        for k in range(SCAN_LEVELS):
            sh = 1 << k
            q_re = blk_re_ref[k:k + 1, sl]
            q_im = blk_im_ref[k:k + 1, sl]
            r_re = jnp.where(blk_row >= sh, pltpu.roll(c_re, sh, 0), 0.0)
            r_im = jnp.where(blk_row >= sh, pltpu.roll(c_im, sh, 0), 0.0)
            c_re, c_im = c_re + (q_re * r_re - q_im * r_im), c_im + (q_re * r_im + q_im * r_re)
        w_re = w[:, :half].reshape(nb, SCAN_ROWS, half)
        w_im = w[:, half:].reshape(nb, SCAN_ROWS, half)
        s_re = w_re + c_re[:, None, :]
        s_im = w_im + c_im[:, None, :]
        last = jnp.concatenate([s_re[nb - 1, SCAN_ROWS - 1:, :], s_im[nb - 1, SCAN_ROWS - 1:, :]], axis=1)
        for c in range(n_ch):
            w_ref[j * n_ch + c, SCAN_ROWS - 1:SCAN_ROWS, :] = last[:, c * LANES:(c + 1) * LANES]
        p_re = pos_re_ref[:, sl][None]
        p_im = pos_im_ref[:, sl][None]
        s_re = s_re.astype(BF16)
        s_im = s_im.astype(BF16)
        h_re = (p_re * s_re - p_im * s_im).reshape(ts, half)
        h_im = (p_re * s_im + p_im * s_re).reshape(ts, half)
        h = jnp.concatenate([h_re, h_im], axis=1)
        ys.append(_bdot(h, cout_ref[j]))
    y = jnp.concatenate(ys, axis=1) + d_ref[...] * u
    y = jax.nn.gelu(y)
    y = y * _sigmoid(_bdot(y.astype(BF16), wglu_ref[...]) + bglu_ref[...])
    ys_ref[...] = y.astype(ys_ref.dtype)

    up = up_ref[...]
    ext = jnp.concatenate([halo_ref[...], up], axis=0)
    halo_ref[...] = up[ts - POOL_HALO:, :]
    pg = pw_ref.shape[1]
    t_pos = (ti * ts + lax.broadcasted_iota(jnp.int32, (ts, 1), 0) + 1).astype(F32)
    outs = []
    for gi, win in enumerate(POOL_WINDOWS):
        e = ext[:, gi * pg:(gi + 1) * pg]
        s = e
        sh = 1
        while sh < win:
            s = s + pltpu.roll(s, sh, 0)
            sh *= 2
        mean = s[POOL_HALO:, :] / jnp.minimum(t_pos, float(win))
        dlt = (mean - e[POOL_HALO:, :]).astype(BF16)
        outs.append(_bdot(dlt, pw_ref[gi]))
    yp = jnp.concatenate(outs, axis=1) * ps_ref[...]
    yp_ref[...] = yp.astype(yp_ref.dtype)


def _mix_mid(us, up, batch, bin_blk, cout_blk, tables, tri, ssm_d, w_glu, b_glu, pool_w, pool_scale):
    t, sw = us.shape
    pw = up.shape[1]
    seq = t // batch
    ts = TS_SCAN
    nt = seq // ts
    n_state = tables[0].shape[1]

    def full(a):
        return pl.BlockSpec(a.shape, lambda b, i, n=a.ndim: (0,) * n)

    row = lambda b, i: (b * nt + i, 0)
    consts = (bin_blk, cout_blk) + tuple(tables) + (tri, ssm_d, w_glu, b_glu, pool_w, pool_scale)
    return pl.pallas_call(
        _mix_mid_kernel,
        grid=(batch, nt),
        in_specs=[pl.BlockSpec((ts, sw), row), pl.BlockSpec((ts, pw), row)] + [full(a) for a in consts],
        out_specs=[pl.BlockSpec((ts, sw), row), pl.BlockSpec((ts, pw), row)],
        out_shape=[jax.ShapeDtypeStruct((t, sw), BF16), jax.ShapeDtypeStruct((t, pw), BF16)],
        scratch_shapes=[pltpu.VMEM((2 * n_state // LANES, SCAN_ROWS + ts, LANES), F32),
                        pltpu.VMEM((POOL_HALO, pw), F32)],
        compiler_params=_params("parallel", "arbitrary"),
        name="mix_mid",
    )(us, up, *consts)


def _mix_out_kernel(x_ref, ys_ref, yp_ref, g_ref, wsu_ref, wpu_ref, wo_ref, lg_ref, lb_ref, o_ref, *, alpha):
    d = x_ref.shape[1]
    y_ssm = _bdot(ys_ref[...], wsu_ref[...])
    y_pool = _bdot(yp_ref[...], wpu_ref[...])
    g = g_ref[...].astype(F32)
    comb = g[:, :d] * y_ssm + g[:, d:] * y_pool
    y = _bdot(comb.astype(BF16), wo_ref[...])
    o_ref[...] = _layer_norm(alpha * x_ref[...] + y, lg_ref[...], lb_ref[...])


def _mix_out(x, ys, yp, gates, w_ssm_up, w_pool_up, w_out, ln_g, ln_b, alpha):
    t, d = x.shape
    tm = TM_PROJ
    row = lambda i: (i, 0)
    full = lambda a: pl.BlockSpec(a.shape, lambda i, n=a.ndim: (0,) * n)
    return pl.pallas_call(
        functools.partial(_mix_out_kernel, alpha=alpha),
        grid=(t // tm,),
        in_specs=[pl.BlockSpec((tm, d), row), pl.BlockSpec((tm, ys.shape[1]), row),
                  pl.BlockSpec((tm, yp.shape[1]), row), pl.BlockSpec((tm, gates.shape[1]), row),
                  full(w_ssm_up), full(w_pool_up), full(w_out), full(ln_g), full(ln_b)],
        out_specs=pl.BlockSpec((tm, d), row),
        out_shape=jax.ShapeDtypeStruct((t, d), F32),
        compiler_params=_params("parallel"),
        name="mix_out",
    )(x, ys, yp, gates, w_ssm_up, w_pool_up, w_out, ln_g, ln_b)


def _kv_kernel(m_ref, wk_ref, wv_ref, k_ref, v_ref):
    mb = m_ref[...].astype(BF16)
    k_ref[...] = _bdot(mb, wk_ref[...]).astype(BF16)
    v_ref[...] = _bdot(mb, wv_ref[...]).astype(BF16)


def _kv_proj(mem2d, wk, wv):
    r, d = mem2d.shape
    tm = 256
    full = lambda a: pl.BlockSpec(a.shape, lambda i: (0, 0))
    return pl.pallas_call(
        _kv_kernel,
        grid=(r // tm,),
        in_specs=[pl.BlockSpec((tm, d), lambda i: (i, 0)), full(wk), full(wv)],
        out_specs=[pl.BlockSpec((tm, d), lambda i: (i, 0))] * 2,
        out_shape=[jax.ShapeDtypeStruct((r, d), BF16)] * 2,
        compiler_params=_params("parallel"),
        name="xattn_kv",
    )(mem2d, wk, wv)


def _xattn_kernel(x_ref, k_ref, v_ref, wq_ref, wo_ref, lg_ref, lb_ref, o_ref, *, alpha):
    x = x_ref[...]
    d = x.shape[1]
    hd = d // N_XHEADS
    q = _bdot(x.astype(BF16), wq_ref[...]) * (hd ** -0.5)
    heads = []
    for h in range(N_XHEADS):
        qh = q[:, h * hd:(h + 1) * hd].astype(BF16)
        kh = k_ref[:, h * hd:(h + 1) * hd]
        s = lax.dot_general(qh, kh, (((1,), (1,)), ((), ())), preferred_element_type=F32)
        s = s - jnp.max(s, axis=-1, keepdims=True)
        p = jnp.exp(s)
        p = p / jnp.sum(p, axis=-1, keepdims=True)
        heads.append(_bdot(p.astype(BF16), v_ref[:, h * hd:(h + 1) * hd]))
    att = jnp.concatenate(heads, axis=1).astype(BF16)
    y = _bdot(att, wo_ref[...])
    o_ref[...] = _layer_norm(alpha * x + y, lg_ref[...], lb_ref[...])


def _xattn(x, k, v, batch, wq, wo, ln_g, ln_b, alpha):
    t, d = x.shape
    n_mem = k.shape[0] // batch
    tm = TM_ATTN
    nt = t // batch // tm
    full = lambda a: pl.BlockSpec(a.shape, lambda b, i, n=a.ndim: (0,) * n)
    row = lambda b, i: (b * nt + i, 0)
    return pl.pallas_call(
        functools.partial(_xattn_kernel, alpha=alpha),
        grid=(batch, nt),
        in_specs=[pl.BlockSpec((tm, d), row),
                  pl.BlockSpec((n_mem, d), lambda b, i: (b, 0)), pl.BlockSpec((n_mem, d), lambda b, i: (b, 0)),
                  full(wq), full(wo), full(ln_g), full(ln_b)],
        out_specs=pl.BlockSpec((tm, d), row),
        out_shape=jax.ShapeDtypeStruct((t, d), F32),
        compiler_params=_params("parallel", "parallel"),
        name="xattn",
    )(x, k, v, wq, wo, ln_g, ln_b)


def _ffn_kernel(x_ref, w1_ref, w3_ref, w2_ref, lg_ref, lb_ref, o_ref, acc_ref, *, alpha):
    f = pl.program_id(1)
    xb = x_ref[...].astype(BF16)
    a = _bdot(xb, w1_ref[...])
    h = (a * _sigmoid(a) * _bdot(xb, w3_ref[...])).astype(BF16)
    part = _bdot(h, w2_ref[...])

    @pl.when(f == 0)
    def _():
        acc_ref[...] = part

    @pl.when(f > 0)
    def _():
        acc_ref[...] += part

    @pl.when(f == pl.num_programs(1) - 1)
    def _():
        o_ref[...] = _layer_norm(alpha * x_ref[...] + acc_ref[...], lg_ref[...], lb_ref[...])


def _ffn_tile(ff):
    for cand in (512, 1408, 1024, 768, 256, 128):
        if ff % cand == 0:
            return cand
    return ff


def _ffn(x, w1, w3, w2, ln_g, ln_b, alpha):
    t, d = x.shape
    ff = w1.shape[1]
    tm, tf = TM_FFN, _ffn_tile(ff)
    full = lambda a: pl.BlockSpec(a.shape, lambda i, f, n=a.ndim: (0,) * n)
    return pl.pallas_call(
        functools.partial(_ffn_kernel, alpha=alpha),
        grid=(t // tm, ff // tf),
        in_specs=[pl.BlockSpec((tm, d), lambda i, f: (i, 0)),
                  pl.BlockSpec((d, tf), lambda i, f: (0, f)), pl.BlockSpec((d, tf), lambda i, f: (0, f)),
                  pl.BlockSpec((tf, d), lambda i, f: (f, 0)), full(ln_g), full(ln_b)],
        out_specs=pl.BlockSpec((tm, d), lambda i, f: (i, 0)),
        out_shape=jax.ShapeDtypeStruct((t, d), F32),
        scratch_shapes=[pltpu.VMEM((tm, d), F32)],
        compiler_params=_params("parallel", "arbitrary"),
        name="ffn_dense",
    )(x, w1, w3, w2, ln_g, ln_b)


def _router_kernel(x_ref, r_ref, o_ref):
    logits = jnp.dot(x_ref[...], r_ref[...], preferred_element_type=F32, precision=lax.Precision.HIGHEST)
    lane = lax.broadcasted_iota(jnp.int32, logits.shape, 1)
    lane_f = lane.astype(F32)
    neg = jnp.float32(-jnp.inf)
    lg = jnp.where(lane < N_EXPERTS, logits, neg)
    m1 = jnp.max(lg, axis=-1, keepdims=True)
    i1 = jnp.min(jnp.where(lg == m1, lane_f, float(LANES)), axis=-1, keepdims=True)
    lg2 = jnp.where(lane_f == i1, neg, lg)
    m2 = jnp.max(lg2, axis=-1, keepdims=True)
    i2 = jnp.min(jnp.where(lg2 == m2, lane_f, float(LANES)), axis=-1, keepdims=True)
    e = jnp.exp(m2 - m1)
    w1 = 1.0 / (1.0 + e)
    w2 = e / (1.0 + e)
    out = jnp.where(lane == 0, w1, jnp.where(lane == 1, w2, jnp.where(lane == 2, i1, jnp.where(lane == 3, i2, 0.0))))
    o_ref[...] = out


def _router(x, router_pad):
    t, d = x.shape
    tm = TM_ROUTER
    return pl.pallas_call(
        _router_kernel,
        grid=(t // tm,),
        in_specs=[pl.BlockSpec((tm, d), lambda i: (i, 0)), pl.BlockSpec(router_pad.shape, lambda i: (0, 0))],
        out_specs=pl.BlockSpec((tm, LANES), lambda i: (i, 0)),
        out_shape=jax.ShapeDtypeStruct((t, LANES), F32),
        compiler_params=_params("parallel"),
        name="moe_router",
    )(x, router_pad)


def _expert_kernel(te_ref, tu_ref, src_hbm, dst_hbm, x_hbm, w1_ref, w3_ref, w2_ref, y_hbm,
                   src_smem, dst_smem, xbuf_ref, xb_ref, obuf_ref, src_sem, dst_sem, gat_sem, sct_sem):
    i = pl.program_id(0)
    f = pl.program_id(1)
    n_tiles = pl.num_programs(0)
    _, n_f, chunk, d = xbuf_ref.shape
    tm = n_f * chunk
    slot = i % 2
    nxt = (i + 1) % 2

    def idx_rows(s):
        return pl.ds(pl.multiple_of(s * IDX_ROWS, IDX_ROWS), IDX_ROWS)

    def src_copy(tile, s):
        return pltpu.make_async_copy(src_hbm.at[tile], src_smem.at[idx_rows(s)], src_sem)

    def dst_copy(tile, s):
        return pltpu.make_async_copy(dst_hbm.at[tile + 1], dst_smem.at[idx_rows(s)], dst_sem)

    def gather_row(s, c, k):
        tok = src_smem[s * IDX_ROWS + c, k]
        return pltpu.make_async_copy(x_hbm.at[pl.ds(tok, 1)], xbuf_ref.at[s, c, pl.ds(k, 1)], gat_sem.at[s])

    def scatter_row(s, c, k):
        row = dst_smem[s * IDX_ROWS + c, k]
        return pltpu.make_async_copy(obuf_ref.at[s, c, pl.ds(k, 1)], y_hbm.at[pl.ds(row, 1)], sct_sem.at[s])

    def wait_gather(s):
        for c in range(n_f):
            pltpu.make_async_copy(x_hbm.at[pl.ds(0, chunk)], xbuf_ref.at[s, c], gat_sem.at[s]).wait()

    def wait_scatter(s):
        for c in range(n_f):
            pltpu.make_async_copy(obuf_ref.at[s, c], y_hbm.at[pl.ds(0, chunk)], sct_sem.at[s]).wait()

    @pl.when(f == 0)
    def _():
        @pl.when(i == 0)
        def _():
            for cp in (src_copy(0, 0), src_copy(1, 1), dst_copy(-1, 1)):
                cp.start()
                cp.wait()
            obuf_ref[...] = jnp.zeros(obuf_ref.shape, obuf_ref.dtype)
            for c in range(n_f):
                def body(k, carry, c=c):
                    gather_row(0, c, k).start()
                    return carry
                lax.fori_loop(0, chunk, body, 0, unroll=8)

        @pl.when(i > 0)
        def _():
            src_copy(i + 1, nxt).wait()
            dst_copy(i - 1, nxt).wait()
            wait_scatter(slot)

        src_copy(i + 2, slot).start()
        dst_copy(i, slot).start()
        wait_gather(slot)
        xb_ref[...] = xbuf_ref[slot].reshape(tm, d).astype(BF16)

    def step(compute):
        for k in range(chunk):
            gather_row(nxt, f, k).start()
            scatter_row(nxt, f, k).start(priority=k % 2)
        if compute:
            xb = xb_ref[...]
            a = _bdot(xb, w1_ref[...].astype(BF16))
            h = (a * _sigmoid(a) * _bdot(xb, w3_ref[...].astype(BF16))).astype(BF16)
            part = _bdot(h, w2_ref[...].astype(BF16))
            prev = jnp.where(f > 0, obuf_ref[slot].reshape(tm, d), 0.0)
            obuf_ref[slot] = (prev + part).reshape(n_f, chunk, d)

    used = tu_ref[i] == 1
    pl.when(used)(functools.partial(step, True))
    pl.when(jnp.logical_not(used))(functools.partial(step, False))

    @pl.when(f == n_f - 1)
    def _():
        @pl.when(i == n_tiles - 1)
        def _():
            wait_scatter(nxt)
            dst_copy(i, slot).wait()
            for c in range(n_f):
                def body(k, carry, c=c):
                    scatter_row(slot, c, k).start()
                    return carry
                lax.fori_loop(0, chunk, body, 0, unroll=8)
            wait_scatter(slot)
            wait_gather(nxt)
            src_copy(i + 2, slot).wait()


def _experts(x, src3d, dst3d, tile_expert, tile_used, layer, w1, w3, w2):
    n_tiles = tile_expert.shape[0]
    chunk = src3d.shape[2]
    d = x.shape[1]
    ff = w1.shape[3]
    tf = _ffn_tile(ff)
    n_f = ff // tf
    tm = n_f * chunk
    assert 2 <= n_f <= IDX_ROWS and chunk % 8 == 0
    assert src3d.shape == (n_tiles + 2, IDX_ROWS, chunk) and dst3d.shape == (n_tiles + 1, IDX_ROWS, chunk)
    any_spec = pl.BlockSpec(memory_space=pl.ANY)
    grid_spec = pltpu.PrefetchScalarGridSpec(
        num_scalar_prefetch=2,
        grid=(n_tiles, n_f),
        in_specs=[any_spec, any_spec, any_spec,
                  pl.BlockSpec((None, None, d, tf), lambda i, f, te, tu: (layer, te[i], 0, f)),
                  pl.BlockSpec((None, None, d, tf), lambda i, f, te, tu: (layer, te[i], 0, f)),
                  pl.BlockSpec((None, None, tf, d), lambda i, f, te, tu: (layer, te[i], f, 0))],
        out_specs=any_spec,
        scratch_shapes=[pltpu.SMEM((2 * IDX_ROWS, chunk), jnp.int32), pltpu.SMEM((2 * IDX_ROWS, chunk), jnp.int32),
                        pltpu.VMEM((2, n_f, chunk, d), F32),
                        pltpu.VMEM((tm, d), BF16),
                        pltpu.VMEM((2, n_f, chunk, d), F32),
                        pltpu.SemaphoreType.DMA, pltpu.SemaphoreType.DMA,
                        pltpu.SemaphoreType.DMA((2,)), pltpu.SemaphoreType.DMA((2,))],
    )
    return pl.pallas_call(
        _expert_kernel,
        grid_spec=grid_spec,
        out_shape=jax.ShapeDtypeStruct(((n_tiles + 1) * tm, d), F32),
        compiler_params=_params("arbitrary", "arbitrary"),
        name="moe_experts",
    )(tile_expert, tile_used, src3d, dst3d, x, w1, w3, w2)


def _combine_kernel(info_ref, x_ref, y0_ref, y1_ref, lg_ref, lb_ref, o_ref, *, alpha):
    info = info_ref[...]
    y = info[:, 0:1] * y0_ref[...] + info[:, 1:2] * y1_ref[...]
    o_ref[...] = _layer_norm(alpha * x_ref[...] + y, lg_ref[...], lb_ref[...])


def _combine(info, x, y, ln_g, ln_b, alpha):
    t, d = x.shape
    tm = TM_COMBINE
    nt = t // tm
    full = lambda a: pl.BlockSpec(a.shape, lambda i, n=a.ndim: (0,) * n)
    return pl.pallas_call(
        functools.partial(_combine_kernel, alpha=alpha),
        grid=(nt,),
        in_specs=[pl.BlockSpec((tm, LANES), lambda i: (i, 0)), pl.BlockSpec((tm, d), lambda i: (i, 0)),
                  pl.BlockSpec((tm, d), lambda i: (i, 0)), pl.BlockSpec((tm, d), lambda i: (i + nt, 0)),
                  full(ln_g), full(ln_b)],
        out_specs=pl.BlockSpec((tm, d), lambda i: (i, 0)),
        out_shape=jax.ShapeDtypeStruct((t, d), F32),
        compiler_params=_params("parallel"),
        name="moe_combine",
    )(info, x, y, y, ln_g, ln_b)


def _moe(x, router, layer, w1, w3, w2, ln_g, ln_b, alpha):
    t, d = x.shape
    ne = router.shape[1]
    router_pad = jnp.pad(router, ((0, 0), (0, LANES - ne)))
    info = _router(x, router_pad)
    top_idx = info[:, 2:4].astype(jnp.int32).reshape(-1)
    onehot = (top_idx[:, None] == jnp.arange(ne, dtype=jnp.int32)[None, :]).astype(jnp.int32)
    rank = jnp.sum((jnp.cumsum(onehot, axis=0) - onehot) * onehot, axis=1)
    counts = jnp.sum(onehot, axis=0)
    padded = ((counts + TM_EXPERT - 1) // TM_EXPERT) * TM_EXPERT
    ends = jnp.cumsum(padded)
    offs = ends - padded
    pos = offs[top_idx] + rank
    n_tiles = (2 * t + ne * (TM_EXPERT - 1) + TM_EXPERT - 1) // TM_EXPERT
    n_rows = n_tiles * TM_EXPERT
    assign = jnp.full((n_rows,), -1, jnp.int32).at[pos].set(jnp.arange(2 * t, dtype=jnp.int32))
    real = assign >= 0
    pad_rank = jnp.cumsum(1 - real.astype(jnp.int32)) - 1
    src = jnp.where(real, assign // 2, 0)
    dst = jnp.where(real, (assign % 2) * t + assign // 2, 2 * t + pad_rank)
    tile_start = jnp.arange(n_tiles, dtype=jnp.int32) * TM_EXPERT
    tile_used = tile_start < ends[-1]
    tile_expert = jnp.sum((tile_start[:, None] >= ends[None, :]).astype(jnp.int32), axis=1)
    last_used = jnp.maximum(ends[-1] // TM_EXPERT - 1, 0)
    tile_expert = jnp.where(tile_used, jnp.minimum(tile_expert, ne - 1), tile_expert[last_used])
    n_f = w1.shape[3] // _ffn_tile(w1.shape[3])

    def idx_blocks(a):
        a = a.reshape(-1, n_f, TM_EXPERT // n_f)
        return jnp.pad(a, ((0, 0), (0, IDX_ROWS - n_f), (0, 0)))

    spare = n_rows + jnp.arange(TM_EXPERT, dtype=jnp.int32)
    src3d = idx_blocks(jnp.concatenate([src, jnp.zeros((2 * TM_EXPERT,), jnp.int32)]))
    dst3d = idx_blocks(jnp.concatenate([spare, dst]))

    y = _experts(x, src3d, dst3d, tile_expert, tile_used.astype(jnp.int32), layer, w1, w3, w2)
    return _combine(info, x, y, ln_g, ln_b, alpha)


def kernel(x, mem, ln_mix_g, ln_mix_b, w_in, ssm_a_re, ssm_a_im, ssm_log_dt, ssm_b_re, ssm_b_im, ssm_c_re, ssm_c_im, ssm_d, ssm_w_glu, ssm_b_glu, w_ssm_up, pool_w, pool_scale, w_pool_up, w_out, ln_xa_g, ln_xa_b, xa_wq, xa_wk, xa_wv, xa_wo, ln_ffn_g, ln_ffn_b, ffn_w1, ffn_w3, ffn_w2, moe_router, moe_w1, moe_w3, moe_w2):
    batch, seq, d = x.shape
    depth = w_in.shape[0]
    alpha = (2 * depth) ** 0.25
    ssm_w = ssm_d.shape[1]
    pool_width = pool_scale.shape[1]
    t = batch * seq
    assert seq % TS_SCAN == 0 and t % TM_PROJ == 0 and TS_SCAN % SCAN_ROWS == 0

    *ssm_tables, bb_re, bb_im = _ssm_prep(ssm_a_re, ssm_a_im, ssm_log_dt, ssm_b_re, ssm_b_im)
    r = jnp.arange(TS_SCAN)
    tri = ((r[:, None] // SCAN_ROWS == r[None, :] // SCAN_ROWS) & (r[None, :] <= r[:, None])).astype(BF16)
    bf = lambda a: a.astype(BF16)
    row = lambda a: a.reshape(1, -1)

    xf = x.reshape(t, d)
    mem2d = mem.reshape(-1, d)
    for l in range(depth):
        us, up, gates = _mix_in(xf, bf(w_in[l]), ssm_w, pool_width)
        ys, yp = _mix_mid(us, up, batch, _block_diag_in(bb_re[l], bb_im[l]),
                          _block_diag_out(ssm_c_re[l], ssm_c_im[l]),
                          [bf(tab[l]) for tab in ssm_tables[:4]] + [tab[l] for tab in ssm_tables[4:]],
                          tri, row(ssm_d[l]), bf(ssm_w_glu[l]),
                          row(ssm_b_glu[l]), bf(pool_w[l]), row(pool_scale[l]))
        xf = _mix_out(xf, ys, yp, gates, bf(w_ssm_up[l]), bf(w_pool_up[l]), bf(w_out[l]),
                      row(ln_mix_g[l]), row(ln_mix_b[l]), alpha)
        k, v = _kv_proj(mem2d, bf(xa_wk[l]), bf(xa_wv[l]))
        xf = _xattn(xf, k, v, batch, bf(xa_wq[l]), bf(xa_wo[l]), row(ln_xa_g[l]), row(ln_xa_b[l]), alpha)
        if l % 2 == 0:
            i = l // 2
            xf = _ffn(xf, bf(ffn_w1[i]), bf(ffn_w3[i]), bf(ffn_w2[i]), row(ln_ffn_g[l]), row(ln_ffn_b[l]), alpha)
        else:
            i = l // 2
            xf = _moe(xf, moe_router[i], i, moe_w1, moe_w3, moe_w2,
                      row(ln_ffn_g[l]), row(ln_ffn_b[l]), alpha)
    return xf.reshape(batch, seq, d)
```

```python
import functools
import math

import jax
import jax.numpy as jnp
from jax import lax
from jax.experimental import pallas as pl
from jax.experimental.pallas import tpu as pltpu

F32 = jnp.float32
BF16 = jnp.bfloat16

LN_EPS = 1e-5
SSM_GROUP = 16
SSM_STATE = 64
GROUPS_PER_BLOCK = 8
SCAN_ROWS = 16
BLK_TABLE_ROWS = 8
POOL_WINDOWS = (2, 4, 8, 16)
POOL_HALO = 16
N_XHEADS = 4
N_EXPERTS = 8
LANES = 128

VMEM_LIMIT_BYTES = 56 * 1024 * 1024

TM_PROJ = 512
TS_SCAN = 256
SCAN_LEVELS = (TS_SCAN // SCAN_ROWS).bit_length() - 1
assert TS_SCAN == SCAN_ROWS << SCAN_LEVELS and SCAN_LEVELS <= BLK_TABLE_ROWS
TM_ATTN = 512
TM_FFN = 512
TM_ROUTER = 512
TM_EXPERT = 896
IDX_ROWS = 8
TM_COMBINE = 256


def _params(*sem):
    return pltpu.CompilerParams(dimension_semantics=sem, vmem_limit_bytes=VMEM_LIMIT_BYTES)


def _layer_norm(z, g, b):
    mu = jnp.mean(z, axis=-1, keepdims=True)
    zc = z - mu
    var = jnp.mean(zc * zc, axis=-1, keepdims=True)
    return zc * lax.rsqrt(var + LN_EPS) * g + b


def _sigmoid(v):
    return 1.0 / (1.0 + jnp.exp(-v))


def _bdot(a, b):
    return jnp.dot(a, b, preferred_element_type=F32)


def _ssm_prep_kernel(a_re_ref, a_im_ref, log_dt_ref, b_re_ref, b_im_ref,
                     pos_re_ref, pos_im_ref, neg_re_ref, neg_im_ref, blk_re_ref, blk_im_ref, bb_re_ref, bb_im_ref):
    a_re = a_re_ref[...]
    a_im = a_im_ref[...]
    dt = jnp.exp(log_dt_ref[...])
    mag = jnp.exp(a_re * dt)
    lam_re = mag * jnp.cos(a_im * dt)
    lam_im = mag * jnp.sin(a_im * dt)
    den = a_re * a_re + a_im * a_im
    num_re = lam_re - 1.0
    f_re = (num_re * a_re + lam_im * a_im) / den
    f_im = (lam_im * a_re - num_re * a_im) / den
    b_re = b_re_ref[...]
    b_im = b_im_ref[...]
    bb_re_ref[...] = f_re[:, None, :] * b_re - f_im[:, None, :] * b_im
    bb_im_ref[...] = f_re[:, None, :] * b_im + f_im[:, None, :] * b_re
    m2 = lam_re * lam_re + lam_im * lam_im
    inv_re = lam_re / m2
    inv_im = -lam_im / m2
    p_re, p_im = lam_re, lam_im
    n_re, n_im = inv_re, inv_im
    for k in range(SCAN_ROWS):
        pos_re_ref[k] = p_re
        pos_im_ref[k] = p_im
        neg_re_ref[k] = n_re
        neg_im_ref[k] = n_im
        if k + 1 < SCAN_ROWS:
            p_re, p_im = p_re * lam_re - p_im * lam_im, p_re * lam_im + p_im * lam_re
            n_re, n_im = n_re * inv_re - n_im * inv_im, n_re * inv_im + n_im * inv_re
    q_re, q_im = p_re, p_im
    for k in range(blk_re_ref.shape[0]):
        if k < SCAN_LEVELS:
            blk_re_ref[k] = q_re
            blk_im_ref[k] = q_im
            q_re, q_im = q_re * q_re - q_im * q_im, 2.0 * q_re * q_im
        else:
            blk_re_ref[k] = jnp.zeros_like(q_re)
            blk_im_ref[k] = jnp.zeros_like(q_im)


def _ssm_prep(a_re, a_im, log_dt, b_re, b_im):
    nl, g, p = a_re.shape
    c = b_re.shape[-1]
    lg = nl * g
    tab = jax.ShapeDtypeStruct((SCAN_ROWS, lg, p), F32)
    blk = jax.ShapeDtypeStruct((BLK_TABLE_ROWS, lg, p), F32)
    bbs = jax.ShapeDtypeStruct((lg, c, p), F32)
    outs = pl.pallas_call(
        _ssm_prep_kernel,
        out_shape=(tab,) * 4 + (blk, blk, bbs, bbs),
        name="ssm_prep",
    )(a_re.reshape(lg, p), a_im.reshape(lg, p), log_dt.reshape(lg, 1),
      b_re.transpose(0, 1, 3, 2).reshape(lg, c, p), b_im.transpose(0, 1, 3, 2).reshape(lg, c, p))

    def tab_layout(t):
        return t.reshape(t.shape[0], nl, g * p).transpose(1, 0, 2)

    return tuple(tab_layout(t) for t in outs[:6]) + (outs[6].reshape(nl, g, c, p), outs[7].reshape(nl, g, c, p))


def _block_diag_in(bb_re, bb_im):
    g, c, p = bb_re.shape
    nb = g // GROUPS_PER_BLOCK
    eye = jnp.eye(GROUPS_PER_BLOCK, dtype=F32)

    def one(bb):
        bb = bb.reshape(nb, GROUPS_PER_BLOCK, c, p)
        m = bb[:, :, :, None, :] * eye[None, :, None, :, None]
        return m.reshape(nb, GROUPS_PER_BLOCK * c, GROUPS_PER_BLOCK * p)

    return jnp.concatenate([one(bb_re), one(bb_im)], axis=-1).astype(BF16)


def _block_diag_out(c_re, c_im):
    g, c, p = c_re.shape
    nb = g // GROUPS_PER_BLOCK
    eye = jnp.eye(GROUPS_PER_BLOCK, dtype=F32)

    def one(cm):
        cm = cm.reshape(nb, GROUPS_PER_BLOCK, c, p).transpose(0, 1, 3, 2)
        m = cm[:, :, :, None, :] * eye[None, :, None, :, None]
        return m.reshape(nb, GROUPS_PER_BLOCK * p, GROUPS_PER_BLOCK * c)

    return jnp.concatenate([one(c_re), -one(c_im)], axis=1).astype(BF16)


def _mix_in_kernel(x_ref, w_ref, us_ref, up_ref, g_ref, *, ssm_w, pool_w):
    xb = x_ref[...].astype(BF16)
    us_ref[...] = _bdot(xb, w_ref[:, :ssm_w])
    up_ref[...] = _bdot(xb, w_ref[:, ssm_w:ssm_w + pool_w])
    g_ref[...] = _sigmoid(_bdot(xb, w_ref[:, ssm_w + pool_w:])).astype(BF16)


def _mix_in(x, w_in, ssm_w, pool_w):
    t, d = x.shape
    cols = w_in.shape[1]
    gate_w = cols - ssm_w - pool_w
    tm = TM_PROJ
    return pl.pallas_call(
        functools.partial(_mix_in_kernel, ssm_w=ssm_w, pool_w=pool_w),
        grid=(t // tm,),
        in_specs=[pl.BlockSpec((tm, d), lambda i: (i, 0)),
                  pl.BlockSpec((d, cols), lambda i: (0, 0))],
        out_specs=[pl.BlockSpec((tm, ssm_w), lambda i: (i, 0)),
                   pl.BlockSpec((tm, pool_w), lambda i: (i, 0)),
                   pl.BlockSpec((tm, gate_w), lambda i: (i, 0))],
        out_shape=[jax.ShapeDtypeStruct((t, ssm_w), F32),
                   jax.ShapeDtypeStruct((t, pool_w), F32),
                   jax.ShapeDtypeStruct((t, gate_w), BF16)],
        compiler_params=_params("parallel"),
        name="mix_in",
    )(x, w_in)


def _mix_mid_kernel(us_ref, up_ref, bin_ref, cout_ref, pos_re_ref, pos_im_ref, neg_re_ref, neg_im_ref,
                    blk_re_ref, blk_im_ref, tri_ref, d_ref, wglu_ref, bglu_ref, pw_ref, ps_ref,
                    ys_ref, yp_ref, w_ref, halo_ref):
    ti = pl.program_id(1)
    ts = us_ref.shape[0]
    n_blocks = bin_ref.shape[0]
    half = bin_ref.shape[2] // 2
    ch = bin_ref.shape[1]
    nb = ts // SCAN_ROWS

    @pl.when(ti == 0)
    def _():
        w_ref[:, 0:SCAN_ROWS, :] = jnp.zeros((w_ref.shape[0], SCAN_ROWS, LANES), F32)
        halo_ref[...] = jnp.zeros_like(halo_ref)

    u = us_ref[...]
    ub = u.astype(BF16)
    blk_row = lax.broadcasted_iota(jnp.int32, (nb, half), 0)
    ys = []
    for j in range(n_blocks):
        c0 = j * 2 * half
        sl = slice(j * half, (j + 1) * half)
        bu = _bdot(ub[:, j * ch:(j + 1) * ch], bin_ref[j]).astype(BF16)
        bu_re = bu[:, :half].reshape(nb, SCAN_ROWS, half)
        bu_im = bu[:, half:].reshape(nb, SCAN_ROWS, half)
        n_re = neg_re_ref[:, sl][None]
        n_im = neg_im_ref[:, sl][None]
        z_re = (n_re * bu_re - n_im * bu_im).reshape(ts, half)
        z_im = (n_re * bu_im + n_im * bu_re).reshape(ts, half)
        z = jnp.concatenate([z_re, z_im], axis=1)
        w = _bdot(tri_ref[...], z)
        n_ch = 2 * half // LANES
        for c in range(n_ch):
            w_ref[j * n_ch + c, SCAN_ROWS:, :] = w[:, c * LANES:(c + 1) * LANES]
        e = [w_ref[j * n_ch + c, pl.ds(SCAN_ROWS - 1, nb, stride=SCAN_ROWS), :] for c in range(n_ch)]
        e_re = jnp.concatenate(e[:n_ch // 2], axis=1)
        e_im = jnp.concatenate(e[n_ch // 2:], axis=1)
        a_re = blk_re_ref[0:1, sl]
        a_im = blk_im_ref[0:1, sl]
        c_re = a_re * e_re - a_im * e_im
        c_im = a_re * e_im + a_im * e_re
        for k in range(SCAN_LEVELS):
            sh = 1 << k
            q_re = blk_re_ref[k:k + 1, sl]
            q_im = blk_im_ref[k:k + 1, sl]
            r_re = jnp.where(blk_row >= sh, pltpu.roll(c_re, sh, 0), 0.0)
            r_im = jnp.where(blk_row >= sh, pltpu.roll(c_im, sh, 0), 0.0)
            c_re, c_im = c_re + (q_re * r_re - q_im * r_im), c_im + (q_re * r_im + q_im * r_re)
        w_re = w[:, :half].reshape(nb, SCAN_ROWS, half)
        w_im = w[:, half:].reshape(nb, SCAN_ROWS, half)
        s_re = w_re + c_re[:, None, :]
        s_im = w_im + c_im[:, None, :]
        last = jnp.concatenate([s_re[nb - 1, SCAN_ROWS - 1:, :], s_im[nb - 1, SCAN_ROWS - 1:, :]], axis=1)
        for c in range(n_ch):
            w_ref[j * n_ch + c, SCAN_ROWS - 1:SCAN_ROWS, :] = last[:, c * LANES:(c + 1) * LANES]
        p_re = pos_re_ref[:, sl][None]
        p_im = pos_im_ref[:, sl][None]
        s_re = s_re.astype(BF16)
        s_im = s_im.astype(BF16)
        h_re = (p_re * s_re - p_im * s_im).reshape(ts, half)
        h_im = (p_re * s_im + p_im * s_re).reshape(ts, half)
        h = jnp.concatenate([h_re, h_im], axis=1)
        ys.append(_bdot(h, cout_ref[j]))
    y = jnp.concatenate(ys, axis=1) + d_ref[...] * u
    y = jax.nn.gelu(y)
    y = y * _sigmoid(_bdot(y.astype(BF16), wglu_ref[...]) + bglu_ref[...])
    ys_ref[...] = y.astype(ys_ref.dtype)

    up = up_ref[...]
    ext = jnp.concatenate([halo_ref[...], up], axis=0)
    halo_ref[...] = up[ts - POOL_HALO:, :]
    pg = pw_ref.shape[1]
    t_pos = (ti * ts + lax.broadcasted_iota(jnp.int32, (ts, 1), 0) + 1).astype(F32)
    outs = []
    for gi, win in enumerate(POOL_WINDOWS):
        e = ext[:, gi * pg:(gi + 1) * pg]
        s = e
        sh = 1
        while sh < win:
            s = s + pltpu.roll(s, sh, 0)
            sh *= 2
        mean = s[POOL_HALO:, :] / jnp.minimum(t_pos, float(win))
        dlt = (mean - e[POOL_HALO:, :]).astype(BF16)
        outs.append(_bdot(dlt, pw_ref[gi]))
    yp = jnp.concatenate(outs, axis=1) * ps_ref[...]
    yp_ref[...] = yp.astype(yp_ref.dtype)


def _mix_mid(us, up, batch, bin_blk, cout_blk, tables, tri, ssm_d, w_glu, b_glu, pool_w, pool_scale):
    t, sw = us.shape
    pw = up.shape[1]
    seq = t // batch
    ts = TS_SCAN
    nt = seq // ts
    n_state = tables[0].shape[1]

    def full(a):
        return pl.BlockSpec(a.shape, lambda b, i, n=a.ndim: (0,) * n)

    row = lambda b, i: (b * nt + i, 0)
    consts = (bin_blk, cout_blk) + tuple(tables) + (tri, ssm_d, w_glu, b_glu, pool_w, pool_scale)
    return pl.pallas_call(
        _mix_mid_kernel,
        grid=(batch, nt),
        in_specs=[pl.BlockSpec((ts, sw), row), pl.BlockSpec((ts, pw), row)] + [full(a) for a in consts],
        out_specs=[pl.BlockSpec((ts, sw), row), pl.BlockSpec((ts, pw), row)],
        out_shape=[jax.ShapeDtypeStruct((t, sw), BF16), jax.ShapeDtypeStruct((t, pw), BF16)],
        scratch_shapes=[pltpu.VMEM((2 * n_state // LANES, SCAN_ROWS + ts, LANES), F32),
                        pltpu.VMEM((POOL_HALO, pw), F32)],
        compiler_params=_params("parallel", "arbitrary"),
        name="mix_mid",
    )(us, up, *consts)


def _mix_out_kernel(x_ref, ys_ref, yp_ref, g_ref, wsu_ref, wpu_ref, wo_ref, lg_ref, lb_ref, o_ref, *, alpha):
    d = x_ref.shape[1]
    y_ssm = _bdot(ys_ref[...], wsu_ref[...])
    y_pool = _bdot(yp_ref[...], wpu_ref[...])
    g = g_ref[...].astype(F32)
    comb = g[:, :d] * y_ssm + g[:, d:] * y_pool
    y = _bdot(comb.astype(BF16), wo_ref[...])
    o_ref[...] = _layer_norm(alpha * x_ref[...] + y, lg_ref[...], lb_ref[...])


def _mix_out(x, ys, yp, gates, w_ssm_up, w_pool_up, w_out, ln_g, ln_b, alpha):
    t, d = x.shape
    tm = TM_PROJ
    row = lambda i: (i, 0)
    full = lambda a: pl.BlockSpec(a.shape, lambda i, n=a.ndim: (0,) * n)
    return pl.pallas_call(
        functools.partial(_mix_out_kernel, alpha=alpha),
        grid=(t // tm,),
        in_specs=[pl.BlockSpec((tm, d), row), pl.BlockSpec((tm, ys.shape[1]), row),
                  pl.BlockSpec((tm, yp.shape[1]), row), pl.BlockSpec((tm, gates.shape[1]), row),
                  full(w_ssm_up), full(w_pool_up), full(w_out), full(ln_g), full(ln_b)],
        out_specs=pl.BlockSpec((tm, d), row),
        out_shape=jax.ShapeDtypeStruct((t, d), F32),
        compiler_params=_params("parallel"),
        name="mix_out",
    )(x, ys, yp, gates, w_ssm_up, w_pool_up, w_out, ln_g, ln_b)


def _kv_kernel(m_ref, wk_ref, wv_ref, k_ref, v_ref):
    mb = m_ref[...].astype(BF16)
    k_ref[...] = _bdot(mb, wk_ref[...]).astype(BF16)
    v_ref[...] = _bdot(mb, wv_ref[...]).astype(BF16)


def _kv_proj(mem2d, wk, wv):
    r, d = mem2d.shape
    tm = 256
    full = lambda a: pl.BlockSpec(a.shape, lambda i: (0, 0))
    return pl.pallas_call(
        _kv_kernel,
        grid=(r // tm,),
        in_specs=[pl.BlockSpec((tm, d), lambda i: (i, 0)), full(wk), full(wv)],
        out_specs=[pl.BlockSpec((tm, d), lambda i: (i, 0))] * 2,
        out_shape=[jax.ShapeDtypeStruct((r, d), BF16)] * 2,
        compiler_params=_params("parallel"),
        name="xattn_kv",
    )(mem2d, wk, wv)


def _xattn_kernel(x_ref, k_ref, v_ref, wq_ref, wo_ref, lg_ref, lb_ref, o_ref, *, alpha):
    x = x_ref[...]
    d = x.shape[1]
    hd = d // N_XHEADS
    q = _bdot(x.astype(BF16), wq_ref[...]) * (hd ** -0.5)
    heads = []
    for h in range(N_XHEADS):
        qh = q[:, h * hd:(h + 1) * hd].astype(BF16)
        kh = k_ref[:, h * hd:(h + 1) * hd]
        s = lax.dot_general(qh, kh, (((1,), (1,)), ((), ())), preferred_element_type=F32)
        s = s - jnp.max(s, axis=-1, keepdims=True)
        p = jnp.exp(s)
        p = p / jnp.sum(p, axis=-1, keepdims=True)
        heads.append(_bdot(p.astype(BF16), v_ref[:, h * hd:(h + 1) * hd]))
    att = jnp.concatenate(heads, axis=1).astype(BF16)
    y = _bdot(att, wo_ref[...])
    o_ref[...] = _layer_norm(alpha * x + y, lg_ref[...], lb_ref[...])


def _xattn(x, k, v, batch, wq, wo, ln_g, ln_b, alpha):
    t, d = x.shape
    n_mem = k.shape[0] // batch
    tm = TM_ATTN
    nt = t // batch // tm
    full = lambda a: pl.BlockSpec(a.shape, lambda b, i, n=a.ndim: (0,) * n)
    row = lambda b, i: (b * nt + i, 0)
    return pl.pallas_call(
        functools.partial(_xattn_kernel, alpha=alpha),
        grid=(batch, nt),
        in_specs=[pl.BlockSpec((tm, d), row),
                  pl.BlockSpec((n_mem, d), lambda b, i: (b, 0)), pl.BlockSpec((n_mem, d), lambda b, i: (b, 0)),
                  full(wq), full(wo), full(ln_g), full(ln_b)],
        out_specs=pl.BlockSpec((tm, d), row),
        out_shape=jax.ShapeDtypeStruct((t, d), F32),
        compiler_params=_params("parallel", "parallel"),
        name="xattn",
    )(x, k, v, wq, wo, ln_g, ln_b)


def _ffn_kernel(x_ref, w1_ref, w3_ref, w2_ref, lg_ref, lb_ref, o_ref, acc_ref, *, alpha):
    f = pl.program_id(1)
    xb = x_ref[...].astype(BF16)
    a = _bdot(xb, w1_ref[...])
    h = (a * _sigmoid(a) * _bdot(xb, w3_ref[...])).astype(BF16)
    part = _bdot(h, w2_ref[...])

    @pl.when(f == 0)
    def _():
        acc_ref[...] = part

    @pl.when(f > 0)
    def _():
        acc_ref[...] += part

    @pl.when(f == pl.num_programs(1) - 1)
    def _():
        o_ref[...] = _layer_norm(alpha * x_ref[...] + acc_ref[...], lg_ref[...], lb_ref[...])


def _ffn_tile(ff):
    for cand in (512, 1408, 1024, 768, 256, 128):
        if ff % cand == 0:
            return cand
    return ff


def _ffn(x, w1, w3, w2, ln_g, ln_b, alpha):
    t, d = x.shape
    ff = w1.shape[1]
    tm, tf = TM_FFN, _ffn_tile(ff)
    full = lambda a: pl.BlockSpec(a.shape, lambda i, f, n=a.ndim: (0,) * n)
    return pl.pallas_call(
        functools.partial(_ffn_kernel, alpha=alpha),
        grid=(t // tm, ff // tf),
        in_specs=[pl.BlockSpec((tm, d), lambda i, f: (i, 0)),
                  pl.BlockSpec((d, tf), lambda i, f: (0, f)), pl.BlockSpec((d, tf), lambda i, f: (0, f)),
                  pl.BlockSpec((tf, d), lambda i, f: (f, 0)), full(ln_g), full(ln_b)],
        out_specs=pl.BlockSpec((tm, d), lambda i, f: (i, 0)),
        out_shape=jax.ShapeDtypeStruct((t, d), F32),
        scratch_shapes=[pltpu.VMEM((tm, d), F32)],
        compiler_params=_params("parallel", "arbitrary"),
        name="ffn_dense",
    )(x, w1, w3, w2, ln_g, ln_b)


def _router_kernel(x_ref, r_ref, o_ref):
    logits = jnp.dot(x_ref[...], r_ref[...], preferred_element_type=F32, precision=lax.Precision.HIGHEST)
    lane = lax.broadcasted_iota(jnp.int32, logits.shape, 1)
    lane_f = lane.astype(F32)
    neg = jnp.float32(-jnp.inf)
    lg = jnp.where(lane < N_EXPERTS, logits, neg)
    m1 = jnp.max(lg, axis=-1, keepdims=True)
    i1 = jnp.min(jnp.where(lg == m1, lane_f, float(LANES)), axis=-1, keepdims=True)
    lg2 = jnp.where(lane_f == i1, neg, lg)
    m2 = jnp.max(lg2, axis=-1, keepdims=True)
    i2 = jnp.min(jnp.where(lg2 == m2, lane_f, float(LANES)), axis=-1, keepdims=True)
    e = jnp.exp(m2 - m1)
    w1 = 1.0 / (1.0 + e)
    w2 = e / (1.0 + e)
    out = jnp.where(lane == 0, w1, jnp.where(lane == 1, w2, jnp.where(lane == 2, i1, jnp.where(lane == 3, i2, 0.0))))
    o_ref[...] = out


def _router(x, router_pad):
    t, d = x.shape
    tm = TM_ROUTER
    return pl.pallas_call(
        _router_kernel,
        grid=(t // tm,),
        in_specs=[pl.BlockSpec((tm, d), lambda i: (i, 0)), pl.BlockSpec(router_pad.shape, lambda i: (0, 0))],
        out_specs=pl.BlockSpec((tm, LANES), lambda i: (i, 0)),
        out_shape=jax.ShapeDtypeStruct((t, LANES), F32),
        compiler_params=_params("parallel"),
        name="moe_router",
    )(x, router_pad)


def _expert_kernel(te_ref, tu_ref, src_hbm, dst_hbm, x_hbm, w1_ref, w3_ref, w2_ref, y_hbm,
                   src_smem, dst_smem, xbuf_ref, xb_ref, obuf_ref, src_sem, dst_sem, gat_sem, sct_sem):
    i = pl.program_id(0)
    f = pl.program_id(1)
    n_tiles = pl.num_programs(0)
    _, n_f, chunk, d = xbuf_ref.shape
    tm = n_f * chunk
    slot = i % 2
    nxt = (i + 1) % 2

    def idx_rows(s):
        return pl.ds(pl.multiple_of(s * IDX_ROWS, IDX_ROWS), IDX_ROWS)

    def src_copy(tile, s):
        return pltpu.make_async_copy(src_hbm.at[tile], src_smem.at[idx_rows(s)], src_sem)

    def dst_copy(tile, s):
        return pltpu.make_async_copy(dst_hbm.at[tile + 1], dst_smem.at[idx_rows(s)], dst_sem)

    def gather_row(s, c, k):
        tok = src_smem[s * IDX_ROWS + c, k]
        return pltpu.make_async_copy(x_hbm.at[pl.ds(tok, 1)], xbuf_ref.at[s, c, pl.ds(k, 1)], gat_sem.at[s])

    def scatter_row(s, c, k):
        row = dst_smem[s * IDX_ROWS + c, k]
        return pltpu.make_async_copy(obuf_ref.at[s, c, pl.ds(k, 1)], y_hbm.at[pl.ds(row, 1)], sct_sem.at[s])

    def wait_gather(s):
        for c in range(n_f):
            pltpu.make_async_copy(x_hbm.at[pl.ds(0, chunk)], xbuf_ref.at[s, c], gat_sem.at[s]).wait()

    def wait_scatter(s):
        for c in range(n_f):
            pltpu.make_async_copy(obuf_ref.at[s, c], y_hbm.at[pl.ds(0, chunk)], sct_sem.at[s]).wait()

    @pl.when(f == 0)
    def _():
        @pl.when(i == 0)
        def _():
            for cp in (src_copy(0, 0), src_copy(1, 1), dst_copy(-1, 1)):
                cp.start()
                cp.wait()
            obuf_ref[...] = jnp.zeros(obuf_ref.shape, obuf_ref.dtype)
            for c in range(n_f):
                def body(k, carry, c=c):
                    gather_row(0, c, k).start()
                    return carry
                lax.fori_loop(0, chunk, body, 0, unroll=8)

        @pl.when(i > 0)
        def _():
            src_copy(i + 1, nxt).wait()
            dst_copy(i - 1, nxt).wait()
            wait_scatter(slot)

        src_copy(i + 2, slot).start()
        dst_copy(i, slot).start()
        wait_gather(slot)
        xb_ref[...] = xbuf_ref[slot].reshape(tm, d).astype(BF16)

    def step(compute):
        for k in range(chunk):
            gather_row(nxt, f, k).start()
            scatter_row(nxt, f, k).start(priority=k % 2)
        if compute:
            xb = xb_ref[...]
            a = _bdot(xb, w1_ref[...])
            h = (a * _sigmoid(a) * _bdot(xb, w3_ref[...])).astype(BF16)
            part = _bdot(h, w2_ref[...])
            prev = jnp.where(f > 0, obuf_ref[slot].reshape(tm, d), 0.0)
            obuf_ref[slot] = (prev + part).reshape(n_f, chunk, d)

    used = tu_ref[i] == 1
    pl.when(used)(functools.partial(step, True))
    pl.when(jnp.logical_not(used))(functools.partial(step, False))

    @pl.when(f == n_f - 1)
    def _():
        @pl.when(i == n_tiles - 1)
        def _():
            wait_scatter(nxt)
            dst_copy(i, slot).wait()
            for c in range(n_f):
                def body(k, carry, c=c):
                    scatter_row(slot, c, k).start()
                    return carry
                lax.fori_loop(0, chunk, body, 0, unroll=8)
            wait_scatter(slot)
            wait_gather(nxt)
            src_copy(i + 2, slot).wait()


def _experts(x, src3d, dst3d, tile_expert, tile_used, layer, w1, w3, w2):
    n_tiles = tile_expert.shape[0]
    chunk = src3d.shape[2]
    d = x.shape[1]
    ff = w1.shape[3]
    tf = _ffn_tile(ff)
    n_f = ff // tf
    tm = n_f * chunk
    assert 2 <= n_f <= IDX_ROWS and chunk % 8 == 0
    assert src3d.shape == (n_tiles + 2, IDX_ROWS, chunk) and dst3d.shape == (n_tiles + 1, IDX_ROWS, chunk)
    any_spec = pl.BlockSpec(memory_space=pl.ANY)
    grid_spec = pltpu.PrefetchScalarGridSpec(
        num_scalar_prefetch=2,
        grid=(n_tiles, n_f),
        in_specs=[any_spec, any_spec, any_spec,
                  pl.BlockSpec((None, None, d, tf), lambda i, f, te, tu: (layer, te[i], 0, f)),
                  pl.BlockSpec((None, None, d, tf), lambda i, f, te, tu: (layer, te[i], 0, f)),
                  pl.BlockSpec((None, None, tf, d), lambda i, f, te, tu: (layer, te[i], f, 0))],
        out_specs=any_spec,
        scratch_shapes=[pltpu.SMEM((2 * IDX_ROWS, chunk), jnp.int32), pltpu.SMEM((2 * IDX_ROWS, chunk), jnp.int32),
                        pltpu.VMEM((2, n_f, chunk, d), F32),
                        pltpu.VMEM((tm, d), BF16),
                        pltpu.VMEM((2, n_f, chunk, d), F32),
                        pltpu.SemaphoreType.DMA, pltpu.SemaphoreType.DMA,
                        pltpu.SemaphoreType.DMA((2,)), pltpu.SemaphoreType.DMA((2,))],
    )
    return pl.pallas_call(
        _expert_kernel,
        grid_spec=grid_spec,
        out_shape=jax.ShapeDtypeStruct(((n_tiles + 1) * tm, d), F32),
        compiler_params=_params("arbitrary", "arbitrary"),
        name="moe_experts",
    )(tile_expert, tile_used, src3d, dst3d, x, w1, w3, w2)


def _combine_kernel(info_ref, x_ref, y0_ref, y1_ref, lg_ref, lb_ref, o_ref, *, alpha):
    info = info_ref[...]
    y = info[:, 0:1] * y0_ref[...] + info[:, 1:2] * y1_ref[...]
    o_ref[...] = _layer_norm(alpha * x_ref[...] + y, lg_ref[...], lb_ref[...])


def _combine(info, x, y, ln_g, ln_b, alpha):
    t, d = x.shape
    tm = TM_COMBINE
    nt = t // tm
    full = lambda a: pl.BlockSpec(a.shape, lambda i, n=a.ndim: (0,) * n)
    return pl.pallas_call(
        functools.partial(_combine_kernel, alpha=alpha),
        grid=(nt,),
        in_specs=[pl.BlockSpec((tm, LANES), lambda i: (i, 0)), pl.BlockSpec((tm, d), lambda i: (i, 0)),
                  pl.BlockSpec((tm, d), lambda i: (i, 0)), pl.BlockSpec((tm, d), lambda i: (i + nt, 0)),
                  full(ln_g), full(ln_b)],
        out_specs=pl.BlockSpec((tm, d), lambda i: (i, 0)),
        out_shape=jax.ShapeDtypeStruct((t, d), F32),
        compiler_params=_params("parallel"),
        name="moe_combine",
    )(info, x, y, y, ln_g, ln_b)


def _moe(x, router, layer, w1, w3, w2, ln_g, ln_b, alpha):
    t, d = x.shape
    ne = router.shape[1]
    router_pad = jnp.pad(router, ((0, 0), (0, LANES - ne)))
    info = _router(x, router_pad)
    top_idx = info[:, 2:4].astype(jnp.int32).reshape(-1)
    onehot = (top_idx[:, None] == jnp.arange(ne, dtype=jnp.int32)[None, :]).astype(jnp.int32)
    rank = jnp.sum((jnp.cumsum(onehot, axis=0) - onehot) * onehot, axis=1)
    counts = jnp.sum(onehot, axis=0)
    padded = ((counts + TM_EXPERT - 1) // TM_EXPERT) * TM_EXPERT
    ends = jnp.cumsum(padded)
    offs = ends - padded
    pos = offs[top_idx] + rank
    n_tiles = (2 * t + ne * (TM_EXPERT - 1) + TM_EXPERT - 1) // TM_EXPERT
    n_rows = n_tiles * TM_EXPERT
    assign = jnp.full((n_rows,), -1, jnp.int32).at[pos].set(jnp.arange(2 * t, dtype=jnp.int32))
    real = assign >= 0
    pad_rank = jnp.cumsum(1 - real.astype(jnp.int32)) - 1
    src = jnp.where(real, assign // 2, 0)
    dst = jnp.where(real, (assign % 2) * t + assign // 2, 2 * t + pad_rank)
    tile_start = jnp.arange(n_tiles, dtype=jnp.int32) * TM_EXPERT
    tile_used = tile_start < ends[-1]
    tile_expert = jnp.sum((tile_start[:, None] >= ends[None, :]).astype(jnp.int32), axis=1)
    last_used = jnp.maximum(ends[-1] // TM_EXPERT - 1, 0)
    tile_expert = jnp.where(tile_used, jnp.minimum(tile_expert, ne - 1), tile_expert[last_used])
    n_f = w1.shape[3] // _ffn_tile(w1.shape[3])

    def idx_blocks(a):
        a = a.reshape(-1, n_f, TM_EXPERT // n_f)
        return jnp.pad(a, ((0, 0), (0, IDX_ROWS - n_f), (0, 0)))

    spare = n_rows + jnp.arange(TM_EXPERT, dtype=jnp.int32)
    src3d = idx_blocks(jnp.concatenate([src, jnp.zeros((2 * TM_EXPERT,), jnp.int32)]))
    dst3d = idx_blocks(jnp.concatenate([spare, dst]))

    y = _experts(x, src3d, dst3d, tile_expert, tile_used.astype(jnp.int32), layer, w1, w3, w2)
    return _combine(info, x, y, ln_g, ln_b, alpha)


def kernel(x, mem, ln_mix_g, ln_mix_b, w_in, ssm_a_re, ssm_a_im, ssm_log_dt, ssm_b_re, ssm_b_im, ssm_c_re, ssm_c_im, ssm_d, ssm_w_glu, ssm_b_glu, w_ssm_up, pool_w, pool_scale, w_pool_up, w_out, ln_xa_g, ln_xa_b, xa_wq, xa_wk, xa_wv, xa_wo, ln_ffn_g, ln_ffn_b, ffn_w1, ffn_w3, ffn_w2, moe_router, moe_w1, moe_w3, moe_w2):
    batch, seq, d = x.shape
    depth = w_in.shape[0]
    alpha = (2 * depth) ** 0.25
    ssm_w = ssm_d.shape[1]
    pool_width = pool_scale.shape[1]
    t = batch * seq
    assert seq % TS_SCAN == 0 and t % TM_PROJ == 0 and TS_SCAN % SCAN_ROWS == 0

    *ssm_tables, bb_re, bb_im = _ssm_prep(ssm_a_re, ssm_a_im, ssm_log_dt, ssm_b_re, ssm_b_im)
    r = jnp.arange(TS_SCAN)
    tri = ((r[:, None] // SCAN_ROWS == r[None, :] // SCAN_ROWS) & (r[None, :] <= r[:, None])).astype(BF16)
    bf = lambda a: a.astype(BF16)
    row = lambda a: a.reshape(1, -1)

    moe_w1_bf, moe_w3_bf, moe_w2_bf = bf(moe_w1), bf(moe_w3), bf(moe_w2)
    xf = x.reshape(t, d)
    mem2d = mem.reshape(-1, d)
    for l in range(depth):
        us, up, gates = _mix_in(xf, bf(w_in[l]), ssm_w, pool_width)
        ys, yp = _mix_mid(us, up, batch, _block_diag_in(bb_re[l], bb_im[l]),
                          _block_diag_out(ssm_c_re[l], ssm_c_im[l]),
                          [bf(tab[l]) for tab in ssm_tables[:4]] + [tab[l] for tab in ssm_tables[4:]],
                          tri, row(ssm_d[l]), bf(ssm_w_glu[l]),
                          row(ssm_b_glu[l]), bf(pool_w[l]), row(pool_scale[l]))
        xf = _mix_out(xf, ys, yp, gates, bf(w_ssm_up[l]), bf(w_pool_up[l]), bf(w_out[l]),
                      row(ln_mix_g[l]), row(ln_mix_b[l]), alpha)
        k, v = _kv_proj(mem2d, bf(xa_wk[l]), bf(xa_wv[l]))
        xf = _xattn(xf, k, v, batch, bf(xa_wq[l]), bf(xa_wo[l]), row(ln_xa_g[l]), row(ln_xa_b[l]), alpha)
        if l % 2 == 0:
            i = l // 2
            xf = _ffn(xf, bf(ffn_w1[i]), bf(ffn_w3[i]), bf(ffn_w2[i]), row(ln_ffn_g[l]), row(ln_ffn_b[l]), alpha)
        else:
            i = l // 2
            xf = _moe(xf, moe_router[i], i, moe_w1_bf, moe_w3_bf, moe_w2_bf,
                      row(ln_ffn_g[l]), row(ln_ffn_b[l]), alpha)
    return xf.reshape(batch, seq, d)
```

```python
import functools
import math

import jax
import jax.numpy as jnp
from jax import lax
from jax.experimental import pallas as pl
from jax.experimental.pallas import tpu as pltpu

F32 = jnp.float32
BF16 = jnp.bfloat16

LN_EPS = 1e-5
SSM_GROUP = 16
SSM_STATE = 64
GROUPS_PER_BLOCK = 8
SCAN_ROWS = 16
BLK_TABLE_ROWS = 8
POOL_WINDOWS = (2, 4, 8, 16)
POOL_HALO = 16
N_XHEADS = 4
N_EXPERTS = 8
LANES = 128

VMEM_LIMIT_BYTES = 56 * 1024 * 1024

TM_PROJ = 512
TS_SCAN = 256
SCAN_LEVELS = (TS_SCAN // SCAN_ROWS).bit_length() - 1
assert TS_SCAN == SCAN_ROWS << SCAN_LEVELS and SCAN_LEVELS <= BLK_TABLE_ROWS
TM_ATTN = 512
TM_FFN = 512
TM_ROUTER = 1024
TM_EXPERT = 784
IDX_ROWS = 8
TM_COMBINE = 512


def _params(*sem):
    return pltpu.CompilerParams(dimension_semantics=sem, vmem_limit_bytes=VMEM_LIMIT_BYTES)


def _layer_norm(z, g, b):
    mu = jnp.mean(z, axis=-1, keepdims=True)
    zc = z - mu
    var = jnp.mean(zc * zc, axis=-1, keepdims=True)
    return zc * lax.rsqrt(var + LN_EPS) * g + b


def _sigmoid(v):
    return 1.0 / (1.0 + jnp.exp(-v))


def _bdot(a, b):
    return jnp.dot(a, b, preferred_element_type=F32)


def _ssm_prep_kernel(a_re_ref, a_im_ref, log_dt_ref, b_re_ref, b_im_ref,
                     pos_re_ref, pos_im_ref, neg_re_ref, neg_im_ref, blk_re_ref, blk_im_ref, bb_re_ref, bb_im_ref):
    a_re = a_re_ref[...]
    a_im = a_im_ref[...]
    dt = jnp.exp(log_dt_ref[...])
    mag = jnp.exp(a_re * dt)
    lam_re = mag * jnp.cos(a_im * dt)
    lam_im = mag * jnp.sin(a_im * dt)
    den = a_re * a_re + a_im * a_im
    num_re = lam_re - 1.0
    f_re = (num_re * a_re + lam_im * a_im) / den
    f_im = (lam_im * a_re - num_re * a_im) / den
    b_re = b_re_ref[...]
    b_im = b_im_ref[...]
    bb_re_ref[...] = f_re[:, None, :] * b_re - f_im[:, None, :] * b_im
    bb_im_ref[...] = f_re[:, None, :] * b_im + f_im[:, None, :] * b_re
    m2 = lam_re * lam_re + lam_im * lam_im
    inv_re = lam_re / m2
    inv_im = -lam_im / m2
    p_re, p_im = lam_re, lam_im
    n_re, n_im = inv_re, inv_im
    for k in range(SCAN_ROWS):
        pos_re_ref[k] = p_re
        pos_im_ref[k] = p_im
        neg_re_ref[k] = n_re
        neg_im_ref[k] = n_im
        if k + 1 < SCAN_ROWS:
            p_re, p_im = p_re * lam_re - p_im * lam_im, p_re * lam_im + p_im * lam_re
            n_re, n_im = n_re * inv_re - n_im * inv_im, n_re * inv_im + n_im * inv_re
    q_re, q_im = p_re, p_im
    for k in range(blk_re_ref.shape[0]):
        if k < SCAN_LEVELS:
            blk_re_ref[k] = q_re
            blk_im_ref[k] = q_im
            q_re, q_im = q_re * q_re - q_im * q_im, 2.0 * q_re * q_im
        else:
            blk_re_ref[k] = jnp.zeros_like(q_re)
            blk_im_ref[k] = jnp.zeros_like(q_im)


def _ssm_prep(a_re, a_im, log_dt, b_re, b_im):
    nl, g, p = a_re.shape
    c = b_re.shape[-1]
    lg = nl * g
    tab = jax.ShapeDtypeStruct((SCAN_ROWS, lg, p), F32)
    blk = jax.ShapeDtypeStruct((BLK_TABLE_ROWS, lg, p), F32)
    bbs = jax.ShapeDtypeStruct((lg, c, p), F32)
    outs = pl.pallas_call(
        _ssm_prep_kernel,
        out_shape=(tab,) * 4 + (blk, blk, bbs, bbs),
        name="ssm_prep",
    )(a_re.reshape(lg, p), a_im.reshape(lg, p), log_dt.reshape(lg, 1),
      b_re.transpose(0, 1, 3, 2).reshape(lg, c, p), b_im.transpose(0, 1, 3, 2).reshape(lg, c, p))

    def tab_layout(t):
        return t.reshape(t.shape[0], nl, g * p).transpose(1, 0, 2)

    return tuple(tab_layout(t) for t in outs[:6]) + (outs[6].reshape(nl, g, c, p), outs[7].reshape(nl, g, c, p))


def _block_diag_in(bb_re, bb_im):
    g, c, p = bb_re.shape
    nb = g // GROUPS_PER_BLOCK
    eye = jnp.eye(GROUPS_PER_BLOCK, dtype=F32)

    def one(bb):
        bb = bb.reshape(nb, GROUPS_PER_BLOCK, c, p)
        m = bb[:, :, :, None, :] * eye[None, :, None, :, None]
        return m.reshape(nb, GROUPS_PER_BLOCK * c, GROUPS_PER_BLOCK * p)

    return jnp.concatenate([one(bb_re), one(bb_im)], axis=-1).astype(BF16)


def _block_diag_out(c_re, c_im):
    g, c, p = c_re.shape
    nb = g // GROUPS_PER_BLOCK
    eye = jnp.eye(GROUPS_PER_BLOCK, dtype=F32)

    def one(cm):
        cm = cm.reshape(nb, GROUPS_PER_BLOCK, c, p).transpose(0, 1, 3, 2)
        m = cm[:, :, :, None, :] * eye[None, :, None, :, None]
        return m.reshape(nb, GROUPS_PER_BLOCK * p, GROUPS_PER_BLOCK * c)

    return jnp.concatenate([one(c_re), -one(c_im)], axis=1).astype(BF16)


def _mix_in_kernel(x_ref, w_ref, us_ref, up_ref, g_ref, *, ssm_w, pool_w):
    xb = x_ref[...].astype(BF16)
    us_ref[...] = _bdot(xb, w_ref[:, :ssm_w])
    up_ref[...] = _bdot(xb, w_ref[:, ssm_w:ssm_w + pool_w])
    g_ref[...] = _sigmoid(_bdot(xb, w_ref[:, ssm_w + pool_w:])).astype(BF16)


def _mix_in(x, w_in, ssm_w, pool_w):
    t, d = x.shape
    cols = w_in.shape[1]
    gate_w = cols - ssm_w - pool_w
    tm = TM_PROJ
    return pl.pallas_call(
        functools.partial(_mix_in_kernel, ssm_w=ssm_w, pool_w=pool_w),
        grid=(t // tm,),
        in_specs=[pl.BlockSpec((tm, d), lambda i: (i, 0)),
                  pl.BlockSpec((d, cols), lambda i: (0, 0))],
        out_specs=[pl.BlockSpec((tm, ssm_w), lambda i: (i, 0)),
                   pl.BlockSpec((tm, pool_w), lambda i: (i, 0)),
                   pl.BlockSpec((tm, gate_w), lambda i: (i, 0))],
        out_shape=[jax.ShapeDtypeStruct((t, ssm_w), F32),
                   jax.ShapeDtypeStruct((t, pool_w), F32),
                   jax.ShapeDtypeStruct((t, gate_w), BF16)],
        compiler_params=_params("parallel"),
        name="mix_in",
    )(x, w_in)


def _mix_mid_kernel(us_ref, up_ref, bin_ref, cout_ref, pos_re_ref, pos_im_ref, neg_re_ref, neg_im_ref,
                    blk_re_ref, blk_im_ref, tri_ref, d_ref, wglu_ref, bglu_ref, pw_ref, ps_ref,
                    ys_ref, yp_ref, w_ref, halo_ref):
    ti = pl.program_id(1)
    ts = us_ref.shape[0]
    n_blocks = bin_ref.shape[0]
    half = bin_ref.shape[2] // 2
    ch = bin_ref.shape[1]
    nb = ts // SCAN_ROWS

    @pl.when(ti == 0)
    def _():
        w_ref[:, 0:SCAN_ROWS, :] = jnp.zeros((w_ref.shape[0], SCAN_ROWS, LANES), F32)
        halo_ref[...] = jnp.zeros_like(halo_ref)

    u = us_ref[...]
    ub = u.astype(BF16)
    blk_row = lax.broadcasted_iota(jnp.int32, (nb, half), 0)
    ys = []
    for j in range(n_blocks):
        c0 = j * 2 * half
        sl = slice(j * half, (j + 1) * half)
        bu = _bdot(ub[:, j * ch:(j + 1) * ch], bin_ref[j]).astype(BF16)
        bu_re = bu[:, :half].reshape(nb, SCAN_ROWS, half)
        bu_im = bu[:, half:].reshape(nb, SCAN_ROWS, half)
        n_re = neg_re_ref[:, sl][None]
        n_im = neg_im_ref[:, sl][None]
        z_re = (n_re * bu_re - n_im * bu_im).reshape(ts, half)
        z_im = (n_re * bu_im + n_im * bu_re).reshape(ts, half)
        z = jnp.concatenate([z_re, z_im], axis=1)
        w = _bdot(tri_ref[...], z)
        n_ch = 2 * half // LANES
        for c in range(n_ch):
            w_ref[j * n_ch + c, SCAN_ROWS:, :] = w[:, c * LANES:(c + 1) * LANES]
        e = [w_ref[j * n_ch + c, pl.ds(SCAN_ROWS - 1, nb, stride=SCAN_ROWS), :] for c in range(n_ch)]
        e_re = jnp.concatenate(e[:n_ch // 2], axis=1)
        e_im = jnp.concatenate(e[n_ch // 2:], axis=1)
        a_re = blk_re_ref[0:1, sl]
        a_im = blk_im_ref[0:1, sl]
        c_re = a_re * e_re - a_im * e_im
        c_im = a_re * e_im + a_im * e_re
        for k in range(SCAN_LEVELS):
            sh = 1 << k
            q_re = blk_re_ref[k:k + 1, sl]
            q_im = blk_im_ref[k:k + 1, sl]
            r_re = jnp.where(blk_row >= sh, pltpu.roll(c_re, sh, 0), 0.0)
            r_im = jnp.where(blk_row >= sh, pltpu.roll(c_im, sh, 0), 0.0)
            c_re, c_im = c_re + (q_re * r_re - q_im * r_im), c_im + (q_re * r_im + q_im * r_re)
        w_re = w[:, :half].reshape(nb, SCAN_ROWS, half)
        w_im = w[:, half:].reshape(nb, SCAN_ROWS, half)
        s_re = w_re + c_re[:, None, :]
        s_im = w_im + c_im[:, None, :]
        last = jnp.concatenate([s_re[nb - 1, SCAN_ROWS - 1:, :], s_im[nb - 1, SCAN_ROWS - 1:, :]], axis=1)
        for c in range(n_ch):
            w_ref[j * n_ch + c, SCAN_ROWS - 1:SCAN_ROWS, :] = last[:, c * LANES:(c + 1) * LANES]
        p_re = pos_re_ref[:, sl][None]
        p_im = pos_im_ref[:, sl][None]
        s_re = s_re.astype(BF16)
        s_im = s_im.astype(BF16)
        h_re = (p_re * s_re - p_im * s_im).reshape(ts, half)
        h_im = (p_re * s_im + p_im * s_re).reshape(ts, half)
        h = jnp.concatenate([h_re, h_im], axis=1)
        ys.append(_bdot(h, cout_ref[j]))
    y = jnp.concatenate(ys, axis=1) + d_ref[...] * u
    y = jax.nn.gelu(y)
    y = y * _sigmoid(_bdot(y.astype(BF16), wglu_ref[...]) + bglu_ref[...])
    ys_ref[...] = y.astype(ys_ref.dtype)

    up = up_ref[...]
    ext = jnp.concatenate([halo_ref[...], up], axis=0)
    halo_ref[...] = up[ts - POOL_HALO:, :]
    pg = pw_ref.shape[1]
    t_pos = (ti * ts + lax.broadcasted_iota(jnp.int32, (ts, 1), 0) + 1).astype(F32)
    outs = []
    for gi, win in enumerate(POOL_WINDOWS):
        e = ext[:, gi * pg:(gi + 1) * pg]
        s = e
        sh = 1
        while sh < win:
            s = s + pltpu.roll(s, sh, 0)
            sh *= 2
        mean = s[POOL_HALO:, :] / jnp.minimum(t_pos, float(win))
        dlt = (mean - e[POOL_HALO:, :]).astype(BF16)
        outs.append(_bdot(dlt, pw_ref[gi]))
    yp = jnp.concatenate(outs, axis=1) * ps_ref[...]
    yp_ref[...] = yp.astype(yp_ref.dtype)


def _mix_mid(us, up, batch, bin_blk, cout_blk, tables, tri, ssm_d, w_glu, b_glu, pool_w, pool_scale):
    t, sw = us.shape
    pw = up.shape[1]
    seq = t // batch
    ts = TS_SCAN
    nt = seq // ts
    n_state = tables[0].shape[1]

    def full(a):
        return pl.BlockSpec(a.shape, lambda b, i, n=a.ndim: (0,) * n)

    row = lambda b, i: (b * nt + i, 0)
    consts = (bin_blk, cout_blk) + tuple(tables) + (tri, ssm_d, w_glu, b_glu, pool_w, pool_scale)
    return pl.pallas_call(
        _mix_mid_kernel,
        grid=(batch, nt),
        in_specs=[pl.BlockSpec((ts, sw), row), pl.BlockSpec((ts, pw), row)] + [full(a) for a in consts],
        out_specs=[pl.BlockSpec((ts, sw), row), pl.BlockSpec((ts, pw), row)],
        out_shape=[jax.ShapeDtypeStruct((t, sw), BF16), jax.ShapeDtypeStruct((t, pw), BF16)],
        scratch_shapes=[pltpu.VMEM((2 * n_state // LANES, SCAN_ROWS + ts, LANES), F32),
                        pltpu.VMEM((POOL_HALO, pw), F32)],
        compiler_params=_params("parallel", "arbitrary"),
        name="mix_mid",
    )(us, up, *consts)


def _mix_out_kernel(x_ref, ys_ref, yp_ref, g_ref, wsu_ref, wpu_ref, wo_ref, lg_ref, lb_ref, o_ref, *, alpha):
    d = x_ref.shape[1]
    y_ssm = _bdot(ys_ref[...], wsu_ref[...])
    y_pool = _bdot(yp_ref[...], wpu_ref[...])
    g = g_ref[...].astype(F32)
    comb = g[:, :d] * y_ssm + g[:, d:] * y_pool
    y = _bdot(comb.astype(BF16), wo_ref[...])
    o_ref[...] = _layer_norm(alpha * x_ref[...] + y, lg_ref[...], lb_ref[...])


def _mix_out(x, ys, yp, gates, w_ssm_up, w_pool_up, w_out, ln_g, ln_b, alpha):
    t, d = x.shape
    tm = TM_PROJ
    row = lambda i: (i, 0)
    full = lambda a: pl.BlockSpec(a.shape, lambda i, n=a.ndim: (0,) * n)
    return pl.pallas_call(
        functools.partial(_mix_out_kernel, alpha=alpha),
        grid=(t // tm,),
        in_specs=[pl.BlockSpec((tm, d), row), pl.BlockSpec((tm, ys.shape[1]), row),
                  pl.BlockSpec((tm, yp.shape[1]), row), pl.BlockSpec((tm, gates.shape[1]), row),
                  full(w_ssm_up), full(w_pool_up), full(w_out), full(ln_g), full(ln_b)],
        out_specs=pl.BlockSpec((tm, d), row),
        out_shape=jax.ShapeDtypeStruct((t, d), F32),
        compiler_params=_params("parallel"),
        name="mix_out",
    )(x, ys, yp, gates, w_ssm_up, w_pool_up, w_out, ln_g, ln_b)


def _kv_kernel(m_ref, wk_ref, wv_ref, k_ref, v_ref):
    mb = m_ref[...].astype(BF16)
    k_ref[...] = _bdot(mb, wk_ref[...]).astype(BF16)
    v_ref[...] = _bdot(mb, wv_ref[...]).astype(BF16)


def _kv_proj(mem2d, wk, wv):
    r, d = mem2d.shape
    tm = 256
    full = lambda a: pl.BlockSpec(a.shape, lambda i: (0, 0))
    return pl.pallas_call(
        _kv_kernel,
        grid=(r // tm,),
        in_specs=[pl.BlockSpec((tm, d), lambda i: (i, 0)), full(wk), full(wv)],
        out_specs=[pl.BlockSpec((tm, d), lambda i: (i, 0))] * 2,
        out_shape=[jax.ShapeDtypeStruct((r, d), BF16)] * 2,
        compiler_params=_params("parallel"),
        name="xattn_kv",
    )(mem2d, wk, wv)


def _xattn_kernel(x_ref, k_ref, v_ref, wq_ref, wo_ref, lg_ref, lb_ref, o_ref, *, alpha):
    x = x_ref[...]
    d = x.shape[1]
    hd = d // N_XHEADS
    q = _bdot(x.astype(BF16), wq_ref[...]) * (hd ** -0.5)
    heads = []
    for h in range(N_XHEADS):
        qh = q[:, h * hd:(h + 1) * hd].astype(BF16)
        kh = k_ref[:, h * hd:(h + 1) * hd]
        s = lax.dot_general(qh, kh, (((1,), (1,)), ((), ())), preferred_element_type=F32)
        s = s - jnp.max(s, axis=-1, keepdims=True)
        p = jnp.exp(s)
        p = p / jnp.sum(p, axis=-1, keepdims=True)
        heads.append(_bdot(p.astype(BF16), v_ref[:, h * hd:(h + 1) * hd]))
    att = jnp.concatenate(heads, axis=1).astype(BF16)
    y = _bdot(att, wo_ref[...])
    o_ref[...] = _layer_norm(alpha * x + y, lg_ref[...], lb_ref[...])


def _xattn(x, k, v, batch, wq, wo, ln_g, ln_b, alpha):
    t, d = x.shape
    n_mem = k.shape[0] // batch
    tm = TM_ATTN
    nt = t // batch // tm
    full = lambda a: pl.BlockSpec(a.shape, lambda b, i, n=a.ndim: (0,) * n)
    row = lambda b, i: (b * nt + i, 0)
    return pl.pallas_call(
        functools.partial(_xattn_kernel, alpha=alpha),
        grid=(batch, nt),
        in_specs=[pl.BlockSpec((tm, d), row),
                  pl.BlockSpec((n_mem, d), lambda b, i: (b, 0)), pl.BlockSpec((n_mem, d), lambda b, i: (b, 0)),
                  full(wq), full(wo), full(ln_g), full(ln_b)],
        out_specs=pl.BlockSpec((tm, d), row),
        out_shape=jax.ShapeDtypeStruct((t, d), F32),
        compiler_params=_params("parallel", "parallel"),
        name="xattn",
    )(x, k, v, wq, wo, ln_g, ln_b)


def _ffn_kernel(x_ref, w1_ref, w3_ref, w2_ref, lg_ref, lb_ref, o_ref, acc_ref, *, alpha):
    f = pl.program_id(1)
    xb = x_ref[...].astype(BF16)
    a = _bdot(xb, w1_ref[...])
    h = (a * _sigmoid(a) * _bdot(xb, w3_ref[...])).astype(BF16)
    part = _bdot(h, w2_ref[...])

    @pl.when(f == 0)
    def _():
        acc_ref[...] = part

    @pl.when(f > 0)
    def _():
        acc_ref[...] += part

    @pl.when(f == pl.num_programs(1) - 1)
    def _():
        o_ref[...] = _layer_norm(alpha * x_ref[...] + acc_ref[...], lg_ref[...], lb_ref[...])


def _ffn_tile(ff):
    for cand in (512, 1408, 1024, 768, 256, 128):
        if ff % cand == 0:
            return cand
    return ff


def _ffn(x, w1, w3, w2, ln_g, ln_b, alpha):
    t, d = x.shape
    ff = w1.shape[1]
    tm, tf = TM_FFN, _ffn_tile(ff)
    full = lambda a: pl.BlockSpec(a.shape, lambda i, f, n=a.ndim: (0,) * n)
    return pl.pallas_call(
        functools.partial(_ffn_kernel, alpha=alpha),
        grid=(t // tm, ff // tf),
        in_specs=[pl.BlockSpec((tm, d), lambda i, f: (i, 0)),
                  pl.BlockSpec((d, tf), lambda i, f: (0, f)), pl.BlockSpec((d, tf), lambda i, f: (0, f)),
                  pl.BlockSpec((tf, d), lambda i, f: (f, 0)), full(ln_g), full(ln_b)],
        out_specs=pl.BlockSpec((tm, d), lambda i, f: (i, 0)),
        out_shape=jax.ShapeDtypeStruct((t, d), F32),
        scratch_shapes=[pltpu.VMEM((tm, d), F32)],
        compiler_params=_params("parallel", "arbitrary"),
        name="ffn_dense",
    )(x, w1, w3, w2, ln_g, ln_b)


def _router_kernel(x_ref, r_ref, o_ref):
    logits = jnp.dot(x_ref[...], r_ref[...], preferred_element_type=F32, precision=lax.Precision.HIGHEST)
    lane = lax.broadcasted_iota(jnp.int32, logits.shape, 1)
    lane_f = lane.astype(F32)
    neg = jnp.float32(-jnp.inf)
    lg = jnp.where(lane < N_EXPERTS, logits, neg)
    m1 = jnp.max(lg, axis=-1, keepdims=True)
    i1 = jnp.min(jnp.where(lg == m1, lane_f, float(LANES)), axis=-1, keepdims=True)
    lg2 = jnp.where(lane_f == i1, neg, lg)
    m2 = jnp.max(lg2, axis=-1, keepdims=True)
    i2 = jnp.min(jnp.where(lg2 == m2, lane_f, float(LANES)), axis=-1, keepdims=True)
    e = jnp.exp(m2 - m1)
    w1 = 1.0 / (1.0 + e)
    w2 = e / (1.0 + e)
    out = jnp.where(lane == 0, w1, jnp.where(lane == 1, w2, jnp.where(lane == 2, i1, jnp.where(lane == 3, i2, 0.0))))
    o_ref[...] = out


def _router(x, router_pad):
    t, d = x.shape
    tm = TM_ROUTER
    return pl.pallas_call(
        _router_kernel,
        grid=(t // tm,),
        in_specs=[pl.BlockSpec((tm, d), lambda i: (i, 0)), pl.BlockSpec(router_pad.shape, lambda i: (0, 0))],
        out_specs=pl.BlockSpec((tm, LANES), lambda i: (i, 0)),
        out_shape=jax.ShapeDtypeStruct((t, LANES), F32),
        compiler_params=_params("parallel"),
        name="moe_router",
    )(x, router_pad)


def _expert_kernel(te_ref, tu_ref, src_hbm, dst_hbm, x_hbm, w1_ref, w3_ref, w2_ref, y_hbm,
                   src_smem, dst_smem, xbuf_ref, xb_ref, obuf_ref, src_sem, dst_sem, gat_sem, sct_sem):
    i = pl.program_id(0)
    f = pl.program_id(1)
    n_tiles = pl.num_programs(0)
    _, n_f, chunk, d = xbuf_ref.shape
    tm = n_f * chunk
    slot = i % 2
    nxt = (i + 1) % 2

    def idx_rows(s):
        return pl.ds(pl.multiple_of(s * IDX_ROWS, IDX_ROWS), IDX_ROWS)

    def src_copy(tile, s):
        return pltpu.make_async_copy(src_hbm.at[tile], src_smem.at[idx_rows(s)], src_sem)

    def dst_copy(tile, s):
        return pltpu.make_async_copy(dst_hbm.at[tile + 1], dst_smem.at[idx_rows(s)], dst_sem)

    def gather_row(s, c, k):
        tok = src_smem[s * IDX_ROWS + c, k]
        return pltpu.make_async_copy(x_hbm.at[pl.ds(tok, 1)], xbuf_ref.at[s, c, pl.ds(k, 1)], gat_sem.at[s])

    def scatter_row(s, c, k):
        row = dst_smem[s * IDX_ROWS + c, k]
        return pltpu.make_async_copy(obuf_ref.at[s, c, pl.ds(k, 1)], y_hbm.at[pl.ds(row, 1)], sct_sem.at[s])

    def wait_gather(s):
        for c in range(n_f):
            pltpu.make_async_copy(x_hbm.at[pl.ds(0, chunk)], xbuf_ref.at[s, c], gat_sem.at[s]).wait()

    def wait_scatter(s):
        for c in range(n_f):
            pltpu.make_async_copy(obuf_ref.at[s, c], y_hbm.at[pl.ds(0, chunk)], sct_sem.at[s]).wait()

    @pl.when(f == 0)
    def _():
        @pl.when(i == 0)
        def _():
            for cp in (src_copy(0, 0), src_copy(1, 1), dst_copy(-1, 1)):
                cp.start()
                cp.wait()
            obuf_ref[...] = jnp.zeros(obuf_ref.shape, obuf_ref.dtype)
            for c in range(n_f):
                def body(k, carry, c=c):
                    gather_row(0, c, k).start()
                    return carry
                lax.fori_loop(0, chunk, body, 0, unroll=8)

        @pl.when(i > 0)
        def _():
            src_copy(i + 1, nxt).wait()
            dst_copy(i - 1, nxt).wait()
            wait_scatter(slot)

        src_copy(i + 2, slot).start()
        dst_copy(i, slot).start()
        wait_gather(slot)
        xb_ref[...] = xbuf_ref[slot].reshape(tm, d).astype(BF16)

    def step(compute):
        for k in range(chunk):
            gather_row(nxt, f, k).start()
            scatter_row(nxt, f, k).start(priority=k % 2)
        if compute:
            xb = xb_ref[...]
            a = _bdot(xb, w1_ref[...].astype(BF16))
            h = (a * _sigmoid(a) * _bdot(xb, w3_ref[...].astype(BF16))).astype(BF16)
            part = _bdot(h, w2_ref[...].astype(BF16))
            prev = jnp.where(f > 0, obuf_ref[slot].reshape(tm, d), 0.0)
            obuf_ref[slot] = (prev + part).reshape(n_f, chunk, d)

    used = tu_ref[i] == 1
    pl.when(used)(functools.partial(step, True))
    pl.when(jnp.logical_not(used))(functools.partial(step, False))

    @pl.when(f == n_f - 1)
    def _():
        @pl.when(i == n_tiles - 1)
        def _():
            wait_scatter(nxt)
            dst_copy(i, slot).wait()
            for c in range(n_f):
                def body(k, carry, c=c):
                    scatter_row(slot, c, k).start()
                    return carry
                lax.fori_loop(0, chunk, body, 0, unroll=8)
            wait_scatter(slot)
            wait_gather(nxt)
            src_copy(i + 2, slot).wait()


def _experts(x, src3d, dst3d, tile_expert, tile_used, layer, w1, w3, w2):
    n_tiles = tile_expert.shape[0]
    chunk = src3d.shape[2]
    d = x.shape[1]
    ff = w1.shape[3]
    tf = _ffn_tile(ff)
    n_f = ff // tf
    tm = n_f * chunk
    assert 2 <= n_f <= IDX_ROWS and chunk % 8 == 0
    assert src3d.shape == (n_tiles + 2, IDX_ROWS, chunk) and dst3d.shape == (n_tiles + 1, IDX_ROWS, chunk)
    any_spec = pl.BlockSpec(memory_space=pl.ANY)
    grid_spec = pltpu.PrefetchScalarGridSpec(
        num_scalar_prefetch=2,
        grid=(n_tiles, n_f),
        in_specs=[any_spec, any_spec, any_spec,
                  pl.BlockSpec((None, None, d, tf), lambda i, f, te, tu: (layer, te[i], 0, f)),
                  pl.BlockSpec((None, None, d, tf), lambda i, f, te, tu: (layer, te[i], 0, f)),
                  pl.BlockSpec((None, None, tf, d), lambda i, f, te, tu: (layer, te[i], f, 0))],
        out_specs=any_spec,
        scratch_shapes=[pltpu.SMEM((2 * IDX_ROWS, chunk), jnp.int32), pltpu.SMEM((2 * IDX_ROWS, chunk), jnp.int32),
                        pltpu.VMEM((2, n_f, chunk, d), F32),
                        pltpu.VMEM((tm, d), BF16),
                        pltpu.VMEM((2, n_f, chunk, d), F32),
                        pltpu.SemaphoreType.DMA, pltpu.SemaphoreType.DMA,
                        pltpu.SemaphoreType.DMA((2,)), pltpu.SemaphoreType.DMA((2,))],
    )
    return pl.pallas_call(
        _expert_kernel,
        grid_spec=grid_spec,
        out_shape=jax.ShapeDtypeStruct(((n_tiles + 1) * tm, d), F32),
        compiler_params=_params("arbitrary", "arbitrary"),
        name="moe_experts",
    )(tile_expert, tile_used, src3d, dst3d, x, w1, w3, w2)


def _combine_kernel(info_ref, x_ref, y0_ref, y1_ref, lg_ref, lb_ref, o_ref, *, alpha):
    info = info_ref[...]
    y = info[:, 0:1] * y0_ref[...] + info[:, 1:2] * y1_ref[...]
    o_ref[...] = _layer_norm(alpha * x_ref[...] + y, lg_ref[...], lb_ref[...])


def _combine(info, x, y, ln_g, ln_b, alpha):
    t, d = x.shape
    tm = TM_COMBINE
    nt = t // tm
    full = lambda a: pl.BlockSpec(a.shape, lambda i, n=a.ndim: (0,) * n)
    return pl.pallas_call(
        functools.partial(_combine_kernel, alpha=alpha),
        grid=(nt,),
        in_specs=[pl.BlockSpec((tm, LANES), lambda i: (i, 0)), pl.BlockSpec((tm, d), lambda i: (i, 0)),
                  pl.BlockSpec((tm, d), lambda i: (i, 0)), pl.BlockSpec((tm, d), lambda i: (i + nt, 0)),
                  full(ln_g), full(ln_b)],
        out_specs=pl.BlockSpec((tm, d), lambda i: (i, 0)),
        out_shape=jax.ShapeDtypeStruct((t, d), F32),
        compiler_params=_params("parallel"),
        name="moe_combine",
    )(info, x, y, y, ln_g, ln_b)


def _moe(x, router, layer, w1, w3, w2, ln_g, ln_b, alpha):
    t, d = x.shape
    ne = router.shape[1]
    router_pad = jnp.pad(router, ((0, 0), (0, LANES - ne)))
    info = _router(x, router_pad)
    top_idx = info[:, 2:4].astype(jnp.int32).reshape(-1)
    onehot = (top_idx[:, None] == jnp.arange(ne, dtype=jnp.int32)[None, :]).astype(jnp.int32)
    rank = jnp.sum((jnp.cumsum(onehot, axis=0) - onehot) * onehot, axis=1)
    counts = jnp.sum(onehot, axis=0)
    padded = ((counts + TM_EXPERT - 1) // TM_EXPERT) * TM_EXPERT
    ends = jnp.cumsum(padded)
    offs = ends - padded
    pos = offs[top_idx] + rank
    n_tiles = (2 * t + ne * (TM_EXPERT - 1) + TM_EXPERT - 1) // TM_EXPERT
    n_rows = n_tiles * TM_EXPERT
    assign = jnp.full((n_rows,), -1, jnp.int32).at[pos].set(jnp.arange(2 * t, dtype=jnp.int32))
    real = assign >= 0
    pad_rank = jnp.cumsum(1 - real.astype(jnp.int32)) - 1
    src = jnp.where(real, assign // 2, 0)
    dst = jnp.where(real, (assign % 2) * t + assign // 2, 2 * t + pad_rank)
    tile_start = jnp.arange(n_tiles, dtype=jnp.int32) * TM_EXPERT
    tile_used = tile_start < ends[-1]
    tile_expert = jnp.sum((tile_start[:, None] >= ends[None, :]).astype(jnp.int32), axis=1)
    last_used = jnp.maximum(ends[-1] // TM_EXPERT - 1, 0)
    tile_expert = jnp.where(tile_used, jnp.minimum(tile_expert, ne - 1), tile_expert[last_used])
    n_f = w1.shape[3] // _ffn_tile(w1.shape[3])

    def idx_blocks(a):
        a = a.reshape(-1, n_f, TM_EXPERT // n_f)
        return jnp.pad(a, ((0, 0), (0, IDX_ROWS - n_f), (0, 0)))

    spare = n_rows + jnp.arange(TM_EXPERT, dtype=jnp.int32)
    src3d = idx_blocks(jnp.concatenate([src, jnp.zeros((2 * TM_EXPERT,), jnp.int32)]))
    dst3d = idx_blocks(jnp.concatenate([spare, dst]))

    y = _experts(x, src3d, dst3d, tile_expert, tile_used.astype(jnp.int32), layer, w1, w3, w2)
    return _combine(info, x, y, ln_g, ln_b, alpha)


def kernel(x, mem, ln_mix_g, ln_mix_b, w_in, ssm_a_re, ssm_a_im, ssm_log_dt, ssm_b_re, ssm_b_im, ssm_c_re, ssm_c_im, ssm_d, ssm_w_glu, ssm_b_glu, w_ssm_up, pool_w, pool_scale, w_pool_up, w_out, ln_xa_g, ln_xa_b, xa_wq, xa_wk, xa_wv, xa_wo, ln_ffn_g, ln_ffn_b, ffn_w1, ffn_w3, ffn_w2, moe_router, moe_w1, moe_w3, moe_w2):
    batch, seq, d = x.shape
    depth = w_in.shape[0]
    alpha = (2 * depth) ** 0.25
    ssm_w = ssm_d.shape[1]
    pool_width = pool_scale.shape[1]
    t = batch * seq
    assert seq % TS_SCAN == 0 and t % TM_PROJ == 0 and TS_SCAN % SCAN_ROWS == 0

    *ssm_tables, bb_re, bb_im = _ssm_prep(ssm_a_re, ssm_a_im, ssm_log_dt, ssm_b_re, ssm_b_im)
    r = jnp.arange(TS_SCAN)
    tri = ((r[:, None] // SCAN_ROWS == r[None, :] // SCAN_ROWS) & (r[None, :] <= r[:, None])).astype(BF16)
    bf = lambda a: a.astype(BF16)
    row = lambda a: a.reshape(1, -1)

    xf = x.reshape(t, d)
    mem2d = mem.reshape(-1, d)
    for l in range(depth):
        us, up, gates = _mix_in(xf, bf(w_in[l]), ssm_w, pool_width)
        ys, yp = _mix_mid(us, up, batch, _block_diag_in(bb_re[l], bb_im[l]),
                          _block_diag_out(ssm_c_re[l], ssm_c_im[l]),
                          [bf(tab[l]) for tab in ssm_tables[:4]] + [tab[l] for tab in ssm_tables[4:]],
                          tri, row(ssm_d[l]), bf(ssm_w_glu[l]),
                          row(ssm_b_glu[l]), bf(pool_w[l]), row(pool_scale[l]))
        xf = _mix_out(xf, ys, yp, gates, bf(w_ssm_up[l]), bf(w_pool_up[l]), bf(w_out[l]),
                      row(ln_mix_g[l]), row(ln_mix_b[l]), alpha)
        k, v = _kv_proj(mem2d, bf(xa_wk[l]), bf(xa_wv[l]))
        xf = _xattn(xf, k, v, batch, bf(xa_wq[l]), bf(xa_wo[l]), row(ln_xa_g[l]), row(ln_xa_b[l]), alpha)
        if l % 2 == 0:
            i = l // 2
            xf = _ffn(xf, bf(ffn_w1[i]), bf(ffn_w3[i]), bf(ffn_w2[i]), row(ln_ffn_g[l]), row(ln_ffn_b[l]), alpha)
        else:
            i = l // 2
            xf = _moe(xf, moe_router[i], i, moe_w1, moe_w3, moe_w2,
                      row(ln_ffn_g[l]), row(ln_ffn_b[l]), alpha)
    return xf.reshape(batch, seq, d)
```
